```python
import math
import jax, jax.numpy as jnp
from jax import lax
import numpy as np

D_MODEL = 1024
BATCH = 2
SEQ = 8192
DEPTH = 4
DEC_BATCH = 128
DEC_SEQ = 1
PAST_LEN = 8192
PAGE_SIZE = 128

N_MIXERS = 4
N_MLA = (DEPTH + 3) // N_MIXERS
N_DIFF = (DEPTH + 2) // N_MIXERS
N_MOBA = (DEPTH + 1) // N_MIXERS
N_SWA = DEPTH // N_MIXERS

HEAD_DIM = 64
N_HEADS = D_MODEL // HEAD_DIM
N_KV_HEADS = N_HEADS // 4
GROUP = N_HEADS // N_KV_HEADS
ATTN_SCALE = HEAD_DIM ** -0.5
Q_BLOCK = 128

MLA_Q_LORA = 3 * D_MODEL // 8
MLA_KV_LORA = D_MODEL // 4
MLA_NOPE = 64
MLA_ROPE = 32
MLA_V = 64
MLA_SCALE = (MLA_NOPE + MLA_ROPE) ** -0.5
ROPE_THETA = 10000.0

DIFF_QK = HEAD_DIM // 2
DIFF_V = HEAD_DIM
DIFF_SCALE = DIFF_QK ** -0.5

MOBA_BLOCK = 256
MOBA_TOPK = 3
MOBA_Q_CHUNK = 64

WINDOW = 128
SWA_BLOCK = 128

N_BUCKETS = 32
MAX_DISTANCE = 128

D_FF = ((8 * D_MODEL // 3 + 127) // 128) * 128
CONV_W = 3
EPS = 1e-6
F32 = jnp.float32

kernel_name = 'hybrid_mla_diff_moba_swa_convffn_step'


def rms_norm(x, g):
    xf = x.astype(F32)
    y = xf * lax.rsqrt(jnp.mean(xf * xf, axis=-1, keepdims=True) + EPS)
    return (y * g.astype(F32)).astype(x.dtype)


def apply_rope(x, pos):
    half = x.shape[-1] // 2
    inv = ROPE_THETA ** (-jnp.arange(half, dtype=F32) / half)
    ang = pos.astype(F32)[:, None] * inv[None, :]
    cos = jnp.cos(ang)[:, None, :]
    sin = jnp.sin(ang)[:, None, :]
    xf = x.astype(F32)
    x1, x2 = xf[..., :half], xf[..., half:]
    return jnp.concatenate([x1 * cos - x2 * sin, x2 * cos + x1 * sin], axis=-1).astype(x.dtype)


def rel_bucket(dist):
    n = jnp.maximum(dist, 0)
    max_exact = N_BUCKETS // 2
    nf = jnp.maximum(n, 1).astype(F32)
    large = max_exact + (jnp.log(nf / max_exact) / math.log(MAX_DISTANCE / max_exact)
                         * (N_BUCKETS - max_exact)).astype(jnp.int32)
    large = jnp.minimum(large, N_BUCKETS - 1)
    return jnp.where(n < max_exact, n, large)


def head_bias(dist, rel_bias):
    b = rel_bias[rel_bucket(dist)].astype(F32)
    return jnp.moveaxis(b, -1, 0).reshape((N_KV_HEADS, GROUP) + dist.shape)


def sink_softmax(s, sinks):
    sk = sinks.astype(F32).reshape(N_KV_HEADS, GROUP)[..., None, None]
    sk = jnp.broadcast_to(sk, s.shape[:-1] + (1,))
    return jax.nn.softmax(jnp.concatenate([s, sk], axis=-1), axis=-1)[..., :-1]


def gather_pages(pool, l, page_table):
    g = pool[l, page_table]
    return g.reshape((g.shape[0], g.shape[1] * g.shape[2]) + g.shape[3:])


def to_pages(x):
    return x.reshape((x.shape[0], x.shape[1] // PAGE_SIZE, PAGE_SIZE) + x.shape[2:])


def gqa_project(x, w_qkv):
    B, T, _ = x.shape
    nq = N_HEADS * HEAD_DIM
    nk = N_KV_HEADS * HEAD_DIM
    qkv = x @ w_qkv
    q = qkv[..., :nq].reshape(B, T, N_HEADS, HEAD_DIM)
    k = qkv[..., nq:nq + nk].reshape(B, T, N_KV_HEADS, HEAD_DIM)
    v = qkv[..., nq + nk:].reshape(B, T, N_KV_HEADS, HEAD_DIM)
    return q, k, v


def mla_project(x, pos, w_dq, g_q, w_uq, w_dkv, g_kv):
    cq = rms_norm(x @ w_dq, g_q)
    q = jnp.einsum('btr,rhe->bthe', cq, w_uq)
    q_nope = q[..., :MLA_NOPE]
    q_pe = apply_rope(q[..., MLA_NOPE:], pos)
    ckv = x @ w_dkv
    c_kv = rms_norm(ckv[..., :MLA_KV_LORA], g_kv)
    k_pe = apply_rope(ckv[..., None, MLA_KV_LORA:], pos)[:, :, 0]
    return q_nope, q_pe, c_kv, k_pe


def mla_prompt(x, pos, w_dq, g_q, w_uq, w_dkv, g_kv, w_uk, w_uv, w_o):
    B, S, _ = x.shape
    q_nope, q_pe, c_kv, k_pe = mla_project(x, pos, w_dq, g_q, w_uq, w_dkv, g_kv)
    k_nope = jnp.einsum('bsr,rhn->bshn', c_kv, w_uk)
    v = jnp.einsum('bsr,rhv->bshv', c_kv, w_uv)
    nqb = S // Q_BLOCK

    def block(args):
        qn, qp, qpos = args
        s = (jnp.einsum('bqhn,bkhn->bhqk', qn, k_nope)
             + jnp.einsum('bqhe,bke->bhqk', qp, k_pe)).astype(F32) * MLA_SCALE
        s = jnp.where(pos[None, :] <= qpos[:, None], s, -jnp.inf)
        p = jax.nn.softmax(s, axis=-1).astype(v.dtype)
        return jnp.einsum('bhqk,bkhv->bqhv', p, v)

    def blocks(t):
        return t.reshape((B, nqb, Q_BLOCK) + t.shape[2:]).swapaxes(0, 1)

    o = lax.map(block, (blocks(q_nope), blocks(q_pe), pos.reshape(nqb, Q_BLOCK)))
    o = o.swapaxes(0, 1).reshape(B, S, N_HEADS * MLA_V)
    return o @ w_o, c_kv, k_pe


def mla_sample(x, pos, lat_past, kpe_past, w_dq, g_q, w_uq, w_dkv, g_kv, w_uk, w_uv, w_o):
    B, T, _ = x.shape
    q_nope, q_pe, c_kv, k_pe = mla_project(x, pos, w_dq, g_q, w_uq, w_dkv, g_kv)
    q_lat = jnp.einsum('bthn,rhn->bthr', q_nope, w_uk)
    s_past = (jnp.einsum('bthr,blr->bhtl', q_lat, lat_past)
              + jnp.einsum('bthe,ble->bhtl', q_pe, kpe_past)).astype(F32)
    s_new = (jnp.einsum('bthr,bsr->bhts', q_lat, c_kv)
             + jnp.einsum('bthe,bse->bhts', q_pe, k_pe)).astype(F32)
    s_new = jnp.where(pos[None, :] <= pos[:, None], s_new, -jnp.inf)
    p = jax.nn.softmax(jnp.concatenate([s_past, s_new], axis=-1) * MLA_SCALE, axis=-1).astype(lat_past.dtype)
    L = lat_past.shape[1]
    o_lat = (jnp.einsum('bhtl,blr->bthr', p[..., :L], lat_past)
             + jnp.einsum('bhts,bsr->bthr', p[..., L:], c_kv))
    o = jnp.einsum('bthr,rhv->bthv', o_lat, w_uv).reshape(B, T, N_HEADS * MLA_V)
    return o @ w_o, c_kv, k_pe


def diff_project(x, w_qkv):
    B, T, _ = x.shape
    nq = N_HEADS * 2 * DIFF_QK
    nk = N_KV_HEADS * 2 * DIFF_QK
    qkv = x @ w_qkv
    q = qkv[..., :nq].reshape(B, T, N_KV_HEADS, GROUP, 2, DIFF_QK)
    k = qkv[..., nq:nq + nk].reshape(B, T, N_KV_HEADS, 2, DIFF_QK)
    v = qkv[..., nq + nk:].reshape(B, T, N_KV_HEADS, DIFF_V)
    return q, k, v


def diff_lambda_value(lam_vec, lam_init):
    lv = lam_vec.astype(F32)
    return jnp.exp(jnp.sum(lv[0] * lv[1])) - jnp.exp(jnp.sum(lv[2] * lv[3])) + lam_init


def diff_attend(q, q_pos, k_segs, v_segs, pos_segs, lam, rel_bias):
    scores = []
    for k, kp in zip(k_segs, pos_segs):
        s = jnp.einsum('btkgmd,blkmd->bkgmtl', q, k).astype(F32) * DIFF_SCALE
        dist = q_pos[:, None] - kp[None, :]
        s = jnp.where(dist >= 0, s + head_bias(dist, rel_bias)[:, :, None], -jnp.inf)
        scores.append(s)
    p = jax.nn.softmax(jnp.concatenate(scores, axis=-1), axis=-1)
    a = p[:, :, :, 0] - lam * p[:, :, :, 1]
    out = None
    off = 0
    for v, kp in zip(v_segs, pos_segs):
        n = kp.shape[0]
        term = jnp.einsum('bkgtl,blkv->btkgv', a[..., off:off + n].astype(v.dtype), v)
        out = term if out is None else out + term
        off += n
    return out


def diff_output(o, g_head, lam_init, w_o):
    B, T = o.shape[:2]
    o = rms_norm(o.reshape(B, T, N_HEADS, DIFF_V), g_head) * (1.0 - lam_init)
    return o.reshape(B, T, N_HEADS * DIFF_V) @ w_o


def diff_prompt(x, pos, w_qkv, lam, lam_init, g_head, w_o, rel_bias):
    B, S, _ = x.shape
    q, k, v = diff_project(x, w_qkv)
    nqb = S // Q_BLOCK
    qb = q.reshape((B, nqb, Q_BLOCK) + q.shape[2:]).swapaxes(0, 1)
    o = lax.map(lambda a: diff_attend(a[0], a[1], [k], [v], [pos], lam, rel_bias),
                (qb, pos.reshape(nqb, Q_BLOCK)))
    o = o.swapaxes(0, 1).reshape(B, S, N_KV_HEADS, GROUP, DIFF_V)
    return diff_output(o, g_head, lam_init, w_o), k, v


def diff_sample(x, pos, k_past, v_past, past_pos, w_qkv, lam, lam_init, g_head, w_o, rel_bias):
    q, k, v = diff_project(x, w_qkv)
    k_past = k_past.reshape(k_past.shape[:3] + (2, DIFF_QK))
    o = diff_attend(q, pos, [k_past, k], [v_past, v], [past_pos, pos], lam, rel_bias)
    return diff_output(o, g_head, lam_init, w_o), k, v


def moba_blocks(k_segs, v_segs):
    B = k_segs[0].shape[0]
    L = sum(s.shape[1] for s in k_segs)
    pad = (-L) % MOBA_BLOCK
    zk = jnp.zeros((B, pad, N_KV_HEADS, HEAD_DIM), k_segs[0].dtype)
    zv = jnp.zeros((B, pad, N_KV_HEADS, HEAD_DIM), v_segs[0].dtype)
    kb = jnp.concatenate(list(k_segs) + [zk], axis=1).reshape(B, -1, MOBA_BLOCK, N_KV_HEADS, HEAD_DIM)
    vb = jnp.concatenate(list(v_segs) + [zv], axis=1).reshape(B, -1, MOBA_BLOCK, N_KV_HEADS, HEAD_DIM)
    kmean = jnp.mean(kb.astype(F32), axis=2).astype(kb.dtype)
    return kb, vb, kmean


def moba_attend(q, q_pos, kb, vb, kmean, rel_bias):
    B, T = q.shape[:2]
    nb = kb.shape[1]
    gate = jnp.einsum('btkgd,bnkd->btkgn', q.reshape(B, T, N_KV_HEADS, GROUP, HEAD_DIM), kmean)
    gate = gate.astype(F32).reshape(B, T, N_HEADS, nb)
    q_blk = q_pos // MOBA_BLOCK
    eligible = jnp.arange(nb)[None, :] < q_blk[:, None]
    gate = jnp.where(eligible[None, :, None, :], gate, -jnp.inf)
    _, top = lax.top_k(gate, min(MOBA_TOPK, nb))
    own = jnp.broadcast_to(q_blk[None, :, None, None], (B, T, N_HEADS, 1)).astype(top.dtype)
    sel = jnp.concatenate([top, own], axis=-1)
    valid = jnp.concatenate([top < q_blk[None, :, None, None], jnp.ones(own.shape, bool)], axis=-1)
    b_idx = jnp.arange(B)[:, None, None, None]
    h_kv = (jnp.arange(N_HEADS) // GROUP)[None, None, :, None]
    ks = kb[b_idx, sel, :, h_kv]
    vs = vb[b_idx, sel, :, h_kv]
    s = jnp.einsum('bthd,bthjcd->bthjc', q, ks).astype(F32) * ATTN_SCALE
    k_pos = sel[..., None] * MOBA_BLOCK + jnp.arange(MOBA_BLOCK)
    dist = q_pos[None, :, None, None, None] - k_pos
    bias = rel_bias[rel_bucket(dist), jnp.arange(N_HEADS)[None, None, :, None, None]].astype(F32)
    s = jnp.where(valid[..., None] & (dist >= 0), s + bias, -jnp.inf)
    p = jax.nn.softmax(s.reshape(B, T, N_HEADS, -1), axis=-1).reshape(s.shape).astype(vs.dtype)
    return jnp.einsum('bthjc,bthjcd->bthd', p, vs)


def moba_prompt(x, pos, w_qkv, w_o, rel_bias):
    B, S, _ = x.shape
    q, k, v = gqa_project(x, w_qkv)
    kb, vb, kmean = moba_blocks([k], [v])
    nqc = S // MOBA_Q_CHUNK
    qc = q.reshape(B, nqc, MOBA_Q_CHUNK, N_HEADS, HEAD_DIM).swapaxes(0, 1)
    o = lax.map(lambda a: moba_attend(a[0], a[1], kb, vb, kmean, rel_bias),
                (qc, pos.reshape(nqc, MOBA_Q_CHUNK)))
    o = o.swapaxes(0, 1).reshape(B, S, N_HEADS * HEAD_DIM)
    return o @ w_o, k, v


def moba_sample(x, pos, k_past, v_past, w_qkv, w_o, rel_bias):
    B, T, _ = x.shape
    q, k, v = gqa_project(x, w_qkv)
    kb, vb, kmean = moba_blocks([k_past, k], [v_past, v])
    o = moba_attend(q, pos, kb, vb, kmean, rel_bias).reshape(B, T, N_HEADS * HEAD_DIM)
    return o @ w_o, k, v


def swa_prompt(x, w_qkv, sinks, w_o, rel_bias, w_buf):
    B, S, _ = x.shape
    q, k, v = gqa_project(x, w_qkv)
    nb = S // SWA_BLOCK
    qb = q.reshape(B, nb, SWA_BLOCK, N_KV_HEADS, GROUP, HEAD_DIM)
    kb = k.reshape(B, nb, SWA_BLOCK, N_KV_HEADS, HEAD_DIM)
    vb = v.reshape(B, nb, SWA_BLOCK, N_KV_HEADS, HEAD_DIM)
    shift = ((0, 0), (1, 0), (0, 0), (0, 0), (0, 0))
    kk = jnp.concatenate([jnp.pad(kb, shift)[:, :-1], kb], axis=2)
    vv = jnp.concatenate([jnp.pad(vb, shift)[:, :-1], vb], axis=2)
    s = jnp.einsum('bnqkgd,bnckd->bnkgqc', qb, kk).astype(F32) * ATTN_SCALE
    start = jnp.arange(nb)[:, None] * SWA_BLOCK
    q_pos = start + jnp.arange(SWA_BLOCK)
    k_pos = start - SWA_BLOCK + jnp.arange(2 * SWA_BLOCK)
    dist = q_pos[:, :, None] - k_pos[:, None, :]
    valid = (dist >= 0) & (dist <= WINDOW) & (k_pos[:, None, :] >= 0)
    bias = rel_bias[rel_bucket(dist)].astype(F32).transpose(0, 3, 1, 2)
    bias = bias.reshape(nb, N_KV_HEADS, GROUP, SWA_BLOCK, 2 * SWA_BLOCK)
    s = jnp.where(valid[:, None, None], s + bias, -jnp.inf)
    p = sink_softmax(s, sinks).astype(vv.dtype)
    o = jnp.einsum('bnkgqc,bnckd->bnqkgd', p, vv).reshape(B, S, N_HEADS * HEAD_DIM)
    return o @ w_o, k[:, S - w_buf:], v[:, S - w_buf:]


def swa_sample(x, pos, buf_k, buf_v, past_len, w_qkv, sinks, w_o, rel_bias):
    B, T, _ = x.shape
    q, k, v = gqa_project(x, w_qkv)
    w_buf = buf_k.shape[1]
    kk = jnp.concatenate([buf_k, k], axis=1)
    vv = jnp.concatenate([buf_v, v], axis=1)
    k_pos = past_len - w_buf + jnp.arange(w_buf + T, dtype=jnp.int32)
    s = jnp.einsum('btkgd,blkd->bkgtl', q.reshape(B, T, N_KV_HEADS, GROUP, HEAD_DIM), kk).astype(F32) * ATTN_SCALE
    dist = pos[:, None] - k_pos[None, :]
    valid = (dist >= 0) & (dist <= WINDOW)
    s = jnp.where(valid, s + head_bias(dist, rel_bias), -jnp.inf)
    p = sink_softmax(s, sinks).astype(vv.dtype)
    o = jnp.einsum('bkgtl,blkd->btkgd', p, vv).reshape(B, T, N_HEADS * HEAD_DIM)
    return o @ w_o, kk[:, T:], vv[:, T:]


def conv_ffn(x, prev, w_up, conv_w, conv_b, w_down):
    T = x.shape[1]
    h = x @ w_up
    hp = jnp.concatenate([prev, h], axis=1)
    c = conv_b
    for j in range(CONV_W):
        c = c + conv_w[j] * hp[:, j:j + T]
    g, u = c[..., :D_FF], c[..., D_FF:]
    y = (jax.nn.silu(g) * u) @ w_down
    return y, hp[:, T:]


def setup_inputs(seed: int = 0) -> dict:
    key = jax.random.key(seed)
    keys = iter(jax.random.split(key, 64))

    def normal(shape, scale=1.0):
        return jax.random.normal(next(keys), shape, F32) * scale

    def gain(shape):
        return 1.0 + 0.05 * jax.random.normal(next(keys), shape, F32)

    D = D_MODEL
    n_pages = PAST_LEN // PAGE_SIZE
    n_pool = (5 * DEC_BATCH * n_pages + 3) // 4
    w_buf = min(WINDOW, PAST_LEN)
    perm = jax.random.permutation(next(keys), n_pool)
    page_table = perm[:DEC_BATCH * n_pages].reshape(DEC_BATCH, n_pages).astype(jnp.int32)
    qkv_gqa = (N_HEADS + 2 * N_KV_HEADS) * HEAD_DIM
    qkv_diff = N_HEADS * 2 * DIFF_QK + N_KV_HEADS * (2 * DIFF_QK + DIFF_V)
    x_prompt = normal((BATCH, SEQ, D))
    x_sample = normal((DEC_BATCH, DEC_SEQ, D))
    cache_mla_latent = normal((N_MLA, n_pool, PAGE_SIZE, MLA_KV_LORA))
    cache_mla_krope = normal((N_MLA, n_pool, PAGE_SIZE, MLA_ROPE))
    cache_diff_k = normal((N_DIFF, n_pool, PAGE_SIZE, N_KV_HEADS, 2 * DIFF_QK))
    cache_diff_v = normal((N_DIFF, n_pool, PAGE_SIZE, N_KV_HEADS, DIFF_V))
    cache_moba_k = normal((N_MOBA, n_pool, PAGE_SIZE, N_KV_HEADS, HEAD_DIM))
    cache_moba_v = normal((N_MOBA, n_pool, PAGE_SIZE, N_KV_HEADS, HEAD_DIM))
    state_swa_k = normal((N_SWA, DEC_BATCH, w_buf, N_KV_HEADS, HEAD_DIM))
    state_swa_v = normal((N_SWA, DEC_BATCH, w_buf, N_KV_HEADS, HEAD_DIM))
    state_ffn_conv = normal((DEPTH, DEC_BATCH, CONV_W - 1, 2 * D_FF))
    return {
        'x_prompt': x_prompt,
        'x_sample': x_sample,
        'cache_mla_latent': cache_mla_latent,
        'cache_mla_krope': cache_mla_krope,
        'cache_diff_k': cache_diff_k,
        'cache_diff_v': cache_diff_v,
        'cache_moba_k': cache_moba_k,
        'cache_moba_v': cache_moba_v,
        'state_swa_k': state_swa_k,
        'state_swa_v': state_swa_v,
        'state_ffn_conv': state_ffn_conv,
        'page_table': page_table,
        'rel_bias': normal((N_BUCKETS, N_HEADS), 0.5),
        'norm_mix_g': gain((DEPTH, D)),
        'norm_ffn_g': gain((DEPTH, D)),
        'norm_final_g': gain((D,)),
        'mla_w_dq': normal((N_MLA, D, MLA_Q_LORA), D ** -0.5),
        'mla_g_q': gain((N_MLA, MLA_Q_LORA)),
        'mla_w_uq': normal((N_MLA, MLA_Q_LORA, N_HEADS, MLA_NOPE + MLA_ROPE), MLA_Q_LORA ** -0.5),
        'mla_w_dkv': normal((N_MLA, D, MLA_KV_LORA + MLA_ROPE), D ** -0.5),
        'mla_g_kv': gain((N_MLA, MLA_KV_LORA)),
        'mla_w_uk': normal((N_MLA, MLA_KV_LORA, N_HEADS, MLA_NOPE), MLA_KV_LORA ** -0.5),
        'mla_w_uv': normal((N_MLA, MLA_KV_LORA, N_HEADS, MLA_V), MLA_KV_LORA ** -0.5),
        'mla_w_o': normal((N_MLA, N_HEADS * MLA_V, D), (N_HEADS * MLA_V) ** -0.5),
        'diff_w_qkv': normal((N_DIFF, D, qkv_diff), D ** -0.5),
        'diff_lambda': normal((N_DIFF, 4, DIFF_QK), 0.1),
        'diff_g_head': gain((N_DIFF, DIFF_V)),
        'diff_w_o': normal((N_DIFF, N_HEADS * DIFF_V, D), (N_HEADS * DIFF_V) ** -0.5),
        'moba_w_qkv': normal((N_MOBA, D, qkv_gqa), D ** -0.5),
        'moba_w_o': normal((N_MOBA, N_HEADS * HEAD_DIM, D), (N_HEADS * HEAD_DIM) ** -0.5),
        'swa_w_qkv': normal((N_SWA, D, qkv_gqa), D ** -0.5),
        'swa_sinks': normal((N_SWA, N_HEADS)),
        'swa_w_o': normal((N_SWA, N_HEADS * HEAD_DIM, D), (N_HEADS * HEAD_DIM) ** -0.5),
        'ffn_w_up': normal((DEPTH, D, 2 * D_FF), D ** -0.5),
        'ffn_conv_w': normal((DEPTH, CONV_W, 2 * D_FF), CONV_W ** -0.5),
        'ffn_conv_b': normal((DEPTH, 2 * D_FF), 0.02),
        'ffn_w_down': normal((DEPTH, D_FF, D), D_FF ** -0.5),
    }


def reference(x_prompt, x_sample, cache_mla_latent, cache_mla_krope, cache_diff_k, cache_diff_v,
              cache_moba_k, cache_moba_v, state_swa_k, state_swa_v, state_ffn_conv, page_table,
              rel_bias, norm_mix_g, norm_ffn_g, norm_final_g,
              mla_w_dq, mla_g_q, mla_w_uq, mla_w_dkv, mla_g_kv, mla_w_uk, mla_w_uv, mla_w_o,
              diff_w_qkv, diff_lambda, diff_g_head, diff_w_o,
              moba_w_qkv, moba_w_o, swa_w_qkv, swa_sinks, swa_w_o,
              ffn_w_up, ffn_conv_w, ffn_conv_b, ffn_w_down):
    seq = x_prompt.shape[1]
    dec_seq = x_sample.shape[1]
    past_len = page_table.shape[1] * PAGE_SIZE
    w_buf = state_swa_k.shape[2]
    pos_p = jnp.arange(seq, dtype=jnp.int32)
    pos_s = past_len + jnp.arange(dec_seq, dtype=jnp.int32)
    pos_past = jnp.arange(past_len, dtype=jnp.int32)

    mla_lat_p, mla_lat_s, mla_kpe_p, mla_kpe_s = [], [], [], []
    diff_k_p, diff_k_s, diff_v_p, diff_v_s = [], [], [], []
    moba_k_p, moba_k_s, moba_v_p, moba_v_s = [], [], [], []
    swa_k_p, swa_k_s, swa_v_p, swa_v_s = [], [], [], []
    conv_p, conv_s = [], []

    hp, hs = x_prompt, x_sample
    for i in range(DEPTH):
        kind, l = i % N_MIXERS, i // N_MIXERS
        ap = rms_norm(hp, norm_mix_g[i])
        asm = rms_norm(hs, norm_mix_g[i])
        if kind == 0:
            w = (mla_w_dq[l], mla_g_q[l], mla_w_uq[l], mla_w_dkv[l], mla_g_kv[l], mla_w_uk[l], mla_w_uv[l], mla_w_o[l])
            mp, lat_p, kpe_p = mla_prompt(ap, pos_p, *w)
            ms, lat_s, kpe_s = mla_sample(asm, pos_s, gather_pages(cache_mla_latent, l, page_table),
                                          gather_pages(cache_mla_krope, l, page_table), *w)
            mla_lat_p.append(to_pages(lat_p))
            mla_kpe_p.append(to_pages(kpe_p))
            mla_lat_s.append(lat_s)
            mla_kpe_s.append(kpe_s)
        elif kind == 1:
            lam_init = 0.8 - 0.6 * math.exp(-0.3 * i)
            lam = diff_lambda_value(diff_lambda[l], lam_init)
            mp, k_p, v_p = diff_prompt(ap, pos_p, diff_w_qkv[l], lam, lam_init, diff_g_head[l], diff_w_o[l], rel_bias)
            ms, k_s, v_s = diff_sample(asm, pos_s, gather_pages(cache_diff_k, l, page_table),
                                       gather_pages(cache_diff_v, l, page_table), pos_past,
                                       diff_w_qkv[l], lam, lam_init, diff_g_head[l], diff_w_o[l], rel_bias)
            diff_k_p.append(to_pages(k_p.reshape(k_p.shape[:3] + (2 * DIFF_QK,))))
            diff_v_p.append(to_pages(v_p))
            diff_k_s.append(k_s.reshape(k_s.shape[:3] + (2 * DIFF_QK,)))
            diff_v_s.append(v_s)
        elif kind == 2:
            mp, k_p, v_p = moba_prompt(ap, pos_p, moba_w_qkv[l], moba_w_o[l], rel_bias)
            ms, k_s, v_s = moba_sample(asm, pos_s, gather_pages(cache_moba_k, l, page_table),
                                       gather_pages(cache_moba_v, l, page_table),
                                       moba_w_qkv[l], moba_w_o[l], rel_bias)
            moba_k_p.append(to_pages(k_p))
            moba_v_p.append(to_pages(v_p))
            moba_k_s.append(k_s)
            moba_v_s.append(v_s)
        else:
            mp, bk_p, bv_p = swa_prompt(ap, swa_w_qkv[l], swa_sinks[l], swa_w_o[l], rel_bias, w_buf)
            ms, bk_s, bv_s = swa_sample(asm, pos_s, state_swa_k[l], state_swa_v[l], past_len,
                                        swa_w_qkv[l], swa_sinks[l], swa_w_o[l], rel_bias)
            swa_k_p.append(bk_p)
            swa_v_p.append(bv_p)
            swa_k_s.append(bk_s)
            swa_v_s.append(bv_s)
        hp = hp + mp
        hs = hs + ms
        fp = rms_norm(hp, norm_ffn_g[i])
        fs = rms_norm(hs, norm_ffn_g[i])
        zero_prev = jnp.zeros((fp.shape[0], CONV_W - 1, 2 * D_FF), fp.dtype)
        yp, cp = conv_ffn(fp, zero_prev, ffn_w_up[i], ffn_conv_w[i], ffn_conv_b[i], ffn_w_down[i])
        ys, cs = conv_ffn(fs, state_ffn_conv[i], ffn_w_up[i], ffn_conv_w[i], ffn_conv_b[i], ffn_w_down[i])
        conv_p.append(cp)
        conv_s.append(cs)
        hp = hp + yp
        hs = hs + ys

    y_prompt = rms_norm(hp, norm_final_g)
    y_sample = rms_norm(hs, norm_final_g)
    return (y_prompt, y_sample,
            jnp.stack(mla_lat_p), jnp.stack(mla_lat_s), jnp.stack(mla_kpe_p), jnp.stack(mla_kpe_s),
            jnp.stack(diff_k_p), jnp.stack(diff_k_s), jnp.stack(diff_v_p), jnp.stack(diff_v_s),
            jnp.stack(moba_k_p), jnp.stack(moba_k_s), jnp.stack(moba_v_p), jnp.stack(moba_v_s),
            jnp.stack(swa_k_p), jnp.stack(swa_k_s), jnp.stack(swa_v_p), jnp.stack(swa_v_s),
            jnp.stack(conv_p), jnp.stack(conv_s))
```

```python
import functools
import math

import jax
import jax.numpy as jnp
import numpy as np
from jax import lax
from jax.experimental import pallas as pl
from jax.experimental.pallas import tpu as pltpu

F32 = jnp.float32
BF16 = jnp.bfloat16

LANE = 128
HEAD_DIM = 64
N_HEADS = 16
N_KV = 4
GROUP = 4
PAGE = 128
MLA_Q_LORA = 384
MLA_KV_LORA = 256
MLA_NOPE = 64
MLA_ROPE = 32
MLA_SCALE = (MLA_NOPE + MLA_ROPE) ** -0.5
ATTN_SCALE = HEAD_DIM ** -0.5
DIFF_QK = 32
DIFF_SCALE = DIFF_QK ** -0.5
ROPE_THETA = 10000.0
MOBA_BLOCK = 256
MOBA_TOPK = 3
N_BUCKETS = 32
EPS = 1e-6
NEG = -1e30
VMEM_LIMIT = 48 * 1024 * 1024

T_DIAG, T_PREV, T_SWA_DIAG, T_SWA_PREV, T_DEC, T_SWA_DEC = range(6)
_SHIFTED_TILES = (T_DIAG, T_PREV, T_DEC)


def _cparams(sem, vmem=None):
    return pltpu.CompilerParams(dimension_semantics=sem, vmem_limit_bytes=vmem)


def _nt_dot(a, b):
    return lax.dot_general(a, b, (((1,), (1,)), ((), ())), preferred_element_type=F32)


def _rms(x, g):
    return x * lax.rsqrt(jnp.mean(x * x, axis=-1, keepdims=True) + EPS) * g


def _bucket_np(dist):
    n = np.maximum(dist, 0)
    nf = np.maximum(n, 1).astype(np.float32)
    large = 16 + (np.log(nf / np.float32(16)) / np.float32(math.log(8.0)) * np.float32(16)).astype(np.int32)
    large = np.minimum(large, N_BUCKETS - 1)
    return np.where(n < 16, n, large).astype(np.int32)


def _bucket_tiles():
    i = np.arange(LANE)[:, None]
    c = np.arange(LANE)[None, :]
    diag = np.where(i >= c, _bucket_np(i - c), -1)
    prev = _bucket_np(LANE + i - c)
    swa_prev = np.where(c >= i, _bucket_np(LANE + i - c), -1)
    dec = _bucket_np(LANE - c + 0 * i)
    return np.stack([diag, prev, diag, swa_prev, dec, dec]).astype(np.int32)


def _bias_tiles_body(rb_ref, d_ref, o_ref):
    h = pl.program_id(0)
    far = rb_ref[N_BUCKETS - 1, h]
    for t in range(6):
        d = d_ref[t]
        acc = jnp.zeros((LANE, LANE), F32)
        for b in range(N_BUCKETS):
            acc = jnp.where(d == b, rb_ref[b, h], acc)
        if t in _SHIFTED_TILES:
            acc = acc - far
        o_ref[t, 0] = jnp.where(d < 0, NEG, acc)


def _bias_tiles(rel_bias):
    d = jnp.asarray(_bucket_tiles())
    return pl.pallas_call(
        _bias_tiles_body,
        grid=(N_HEADS,),
        in_specs=[pl.BlockSpec(memory_space=pltpu.SMEM),
                  pl.BlockSpec((6, LANE, LANE), lambda h: (0, 0, 0))],
        out_specs=pl.BlockSpec((6, 1, LANE, LANE), lambda h: (0, h, 0, 0)),
        out_shape=jax.ShapeDtypeStruct((6, N_HEADS, LANE, LANE), F32),
        compiler_params=_cparams(("arbitrary",)),
        name="bias_tiles",
    )(rel_bias.astype(F32), d)


def _proj_body(*refs, n_w, norm, residual):
    it = iter(refs)
    x_ref = next(it)
    g_ref = next(it) if norm else None
    r_ref = next(it) if residual else None
    w_refs = [next(it) for _ in range(n_w)]
    o_refs = [next(it) for _ in range(n_w)]
    x = x_ref[...].astype(F32)
    if norm:
        x = _rms(x, g_ref[...])
    xb = x.astype(BF16)
    for k in range(n_w):
        acc = jnp.dot(xb, w_refs[k][...], preferred_element_type=F32)
        if residual and k == 0:
            acc = acc + r_ref[...]
        o_refs[k][...] = acc.astype(o_refs[k].dtype)


def _proj(x, ws, out_dtypes, g=None, res=None, tm=512, name="proj"):
    t, kdim = x.shape
    tm = min(tm, t)
    args = [x]
    specs = [pl.BlockSpec((tm, kdim), lambda i: (i, 0))]
    if g is not None:
        args.append(g.reshape(1, kdim).astype(F32))
        specs.append(pl.BlockSpec((1, kdim), lambda i: (0, 0)))
    if res is not None:
        args.append(res)
        specs.append(pl.BlockSpec((tm, res.shape[1]), lambda i: (i, 0)))
    for w in ws:
        args.append(w)
        specs.append(pl.BlockSpec(w.shape, lambda i: (0, 0)))
    outs = pl.pallas_call(
        functools.partial(_proj_body, n_w=len(ws), norm=g is not None, residual=res is not None),
        grid=(t // tm,),
        in_specs=specs,
        out_specs=[pl.BlockSpec((tm, w.shape[1]), lambda i: (i, 0)) for w in ws],
        out_shape=[jax.ShapeDtypeStruct((t, w.shape[1]), dt) for w, dt in zip(ws, out_dtypes)],
        compiler_params=_cparams(("arbitrary",), VMEM_LIMIT),
        name=name,
    )(*args)
    return outs


def _norm_body(x_ref, g_ref, o_ref):
    o_ref[...] = _rms(x_ref[...], g_ref[...])


def _final_norm(x, g, tm=512):
    t, d = x.shape
    tm = min(tm, t)
    return pl.pallas_call(
        _norm_body,
        grid=(t // tm,),
        in_specs=[pl.BlockSpec((tm, d), lambda i: (i, 0)), pl.BlockSpec((1, d), lambda i: (0, 0))],
        out_specs=pl.BlockSpec((tm, d), lambda i: (i, 0)),
        out_shape=jax.ShapeDtypeStruct((t, d), F32),
        compiler_params=_cparams(("arbitrary",)),
        name="final_norm",
    )(x, g.reshape(1, d))


def _headwise_body(x_ref, w_ref, o_ref):
    o_ref[...] = jnp.dot(x_ref[...].astype(BF16), w_ref[0], preferred_element_type=F32).astype(o_ref.dtype)


def _headwise(x, w, out_dtype, name):
    m = x.shape[0]
    n, kb, nb = w.shape
    return pl.pallas_call(
        _headwise_body,
        grid=(n,),
        in_specs=[pl.BlockSpec((m, kb), lambda i: (0, i)), pl.BlockSpec((1, kb, nb), lambda i: (i, 0, 0))],
        out_specs=pl.BlockSpec((m, nb), lambda i: (0, i)),
        out_shape=jax.ShapeDtypeStruct((m, n * nb), out_dtype),
        compiler_params=_cparams(("arbitrary",)),
        name=name,
    )(x, w)


def _mla_stage2_body(s1_ref, gq_ref, gkv_ref, cq_ref, sq_ref, ck_ref, wuq_ref, wuk_ref, wuv_ref,
                     q_ref, k_ref, v_ref, lat_ref, kpe_ref):
    s1 = s1_ref[...]
    cosq, sinq, cosk = cq_ref[...], sq_ref[...], ck_ref[...]
    cqn = _rms(s1[:, :MLA_Q_LORA], gq_ref[...]).astype(BF16)
    qa = jnp.dot(cqn, wuq_ref[...], preferred_element_type=F32)
    for h in range(N_HEADS):
        blk = qa[:, h * LANE:(h + 1) * LANE]
        q_ref[:, h * LANE:(h + 1) * LANE] = (blk * cosq + pltpu.roll(blk, LANE - MLA_ROPE, axis=1) * sinq).astype(BF16)
    lat = _rms(s1[:, MLA_Q_LORA:MLA_Q_LORA + MLA_KV_LORA], gkv_ref[...])
    lat_ref[...] = lat
    latb = lat.astype(BF16)
    pe = s1[:, MLA_Q_LORA + MLA_KV_LORA:]
    kpe = pltpu.roll(pe, 2 * MLA_ROPE, axis=1) * cosk + pltpu.roll(pe, MLA_ROPE, axis=1) * sinq
    kpe_ref[...] = kpe
    kn = jnp.dot(latb, wuk_ref[...], preferred_element_type=F32)
    for h in range(N_HEADS):
        k_ref[:, h * LANE:(h + 1) * LANE] = (kn[:, h * LANE:(h + 1) * LANE] + kpe).astype(BF16)
    v_ref[...] = jnp.dot(latb, wuv_ref[...], preferred_element_type=F32).astype(BF16)


def _mla_stage2(s1, g_q, g_kv, cosq, sinq, cosk, wuq, wuk, wuv, tm=256):
    t = s1.shape[0]
    tm = min(tm, t)
    ntab = cosq.shape[0] // tm
    row = lambda i: (i, 0)
    tab = lambda i: (i % ntab, 0)
    full = lambda i: (0, 0)
    hw = N_HEADS * LANE
    return pl.pallas_call(
        _mla_stage2_body,
        grid=(t // tm,),
        in_specs=[pl.BlockSpec((tm, s1.shape[1]), row),
                  pl.BlockSpec((1, MLA_Q_LORA), full), pl.BlockSpec((1, MLA_KV_LORA), full),
                  pl.BlockSpec((tm, LANE), tab), pl.BlockSpec((tm, LANE), tab), pl.BlockSpec((tm, LANE), tab),
                  pl.BlockSpec(wuq.shape, full), pl.BlockSpec(wuk.shape, full), pl.BlockSpec(wuv.shape, full)],
        out_specs=[pl.BlockSpec((tm, hw), row), pl.BlockSpec((tm, hw), row),
                   pl.BlockSpec((tm, N_HEADS * HEAD_DIM), row),
                   pl.BlockSpec((tm, MLA_KV_LORA), row), pl.BlockSpec((tm, LANE), row)],
        out_shape=[jax.ShapeDtypeStruct((t, hw), BF16), jax.ShapeDtypeStruct((t, hw), BF16),
                   jax.ShapeDtypeStruct((t, N_HEADS * HEAD_DIM), BF16),
                   jax.ShapeDtypeStruct((t, MLA_KV_LORA), F32), jax.ShapeDtypeStruct((t, LANE), F32)],
        compiler_params=_cparams(("arbitrary",), VMEM_LIMIT),
        name="mla_stage2",
    )(s1, g_q.reshape(1, -1), g_kv.reshape(1, -1), cosq, sinq, cosk, wuq, wuk, wuv)


def _flash_step(q, k, v, m_ref, l_ref, acc_ref, bias=None):
    width = k.shape[0]
    s = _nt_dot(q, k)
    if bias is not None:
        s = s + bias
    cols = [s[:, c * LANE:(c + 1) * LANE] for c in range(width // LANE)]
    mx = cols[0]
    for c in cols[1:]:
        mx = jnp.maximum(mx, c)
    m_prev = m_ref[...]
    m_new = jnp.maximum(m_prev, jnp.max(mx, axis=1, keepdims=True))
    alpha = jnp.exp(m_prev - m_new)
    ps = [jnp.exp(c - m_new) for c in cols]
    lsum = ps[0]
    for p in ps[1:]:
        lsum = lsum + p
    l_ref[...] = alpha * l_ref[...] + jnp.sum(lsum, axis=1, keepdims=True)
    p = ps[0] if len(ps) == 1 else jnp.concatenate(ps, axis=1)
    acc_ref[...] = alpha * acc_ref[...] + jnp.dot(p.astype(BF16), v, preferred_element_type=F32)
    m_ref[...] = m_new


def _init_flash(m_ref, l_ref, acc_ref):
    m_ref[...] = jnp.full(m_ref.shape, NEG, F32)
    l_ref[...] = jnp.zeros(l_ref.shape, F32)
    acc_ref[...] = jnp.zeros(acc_ref.shape, F32)


def _mla_prompt_body(q_ref, k_ref, v_ref, o_ref, m_sc, l_sc, acc_sc, *, tq):
    qi = pl.program_id(2)
    row = lax.broadcasted_iota(jnp.int32, (tq, tq), 0)
    col = lax.broadcasted_iota(jnp.int32, (tq, tq), 1)
    causal = jnp.where(col <= row, 0.0, NEG)
    outs = []
    for hh in range(2):
        q = q_ref[0, :, hh * LANE:(hh + 1) * LANE]
        m_ref, l_ref, acc_ref = m_sc.at[hh], l_sc.at[hh], acc_sc.at[hh]
        _init_flash(m_ref, l_ref, acc_ref)

        def far(j, carry, hh=hh, q=q, m_ref=m_ref, l_ref=l_ref, acc_ref=acc_ref):
            st = pl.multiple_of(j * tq, tq)
            _flash_step(q, k_ref[0, pl.ds(st, tq), hh * LANE:(hh + 1) * LANE], v_ref[0, pl.ds(st, tq), :],
                        m_ref, l_ref, acc_ref)
            return carry

        lax.fori_loop(0, qi, far, 0)
        st = pl.multiple_of(qi * tq, tq)
        _flash_step(q, k_ref[0, pl.ds(st, tq), hh * LANE:(hh + 1) * LANE], v_ref[0, pl.ds(st, tq), :],
                    m_ref, l_ref, acc_ref, bias=causal)
        outs.append(acc_ref[...] / l_ref[...])
    lane = lax.broadcasted_iota(jnp.int32, (tq, LANE), 1)
    o_ref[0] = jnp.where(lane < HEAD_DIM, outs[0], outs[1]).astype(o_ref.dtype)


def _mla_prompt_attn(qs, ks, v, b, s, tq=256):
    tq = min(tq, s)
    hw = N_HEADS * LANE
    qs, ks = qs.reshape(b, s, hw), ks.reshape(b, s, hw)
    v = v.reshape(b, s, N_HEADS * HEAD_DIM)
    return pl.pallas_call(
        functools.partial(_mla_prompt_body, tq=tq),
        grid=(b, N_HEADS // 2, s // tq),
        in_specs=[pl.BlockSpec((1, tq, 2 * LANE), lambda bi, hp, qi: (bi, qi, hp)),
                  pl.BlockSpec((1, s, 2 * LANE), lambda bi, hp, qi: (bi, 0, hp)),
                  pl.BlockSpec((1, s, LANE), lambda bi, hp, qi: (bi, 0, hp))],
        out_specs=pl.BlockSpec((1, tq, LANE), lambda bi, hp, qi: (bi, qi, hp)),
        out_shape=jax.ShapeDtypeStruct((b, s, N_HEADS * HEAD_DIM), BF16),
        scratch_shapes=[pltpu.VMEM((2, tq, LANE), F32)] * 3,
        compiler_params=_cparams(("arbitrary", "arbitrary", "arbitrary"), VMEM_LIMIT),
        name="mla_prompt_attn",
    )(qs, ks, v)


def _lambda_value(lamv, lam_init):
    a = jnp.sum(lamv[0:1] * lamv[1:2], axis=1, keepdims=True)
    c = jnp.sum(lamv[2:3] * lamv[3:4], axis=1, keepdims=True)
    return jnp.exp(a) - jnp.exp(c) + lam_init


def _gqa_prompt_body(*refs, kind, tq, far_w, lam_init):
    if kind == "diff":
        q_ref, k_ref, v_ref, bias_ref, lamv_ref, gh_ref, o_ref, q_sc, m_sc, l_sc, acc_sc = refs
        nmap = 2
    else:
        q_ref, k_ref, v_ref, bias_ref, o_ref, q_sc, m_sc, l_sc, acc_sc, kmh_sc, kml_sc = refs
        nmap = 1
    qi = pl.program_id(2)
    rows = nmap * GROUP * tq
    lane_t = lax.broadcasted_iota(jnp.int32, (tq, LANE), 1)

    if kind == "diff":
        for mp in range(2):
            keep = (lane_t >= mp * DIFF_QK) & (lane_t < (mp + 1) * DIFF_QK)
            for hh in range(GROUP):
                qh = q_ref[0, :, hh * LANE:(hh + 1) * LANE]
                r0 = (mp * GROUP + hh) * tq
                q_sc[r0:r0 + tq, :] = jnp.where(keep, qh.astype(F32), 0.0).astype(BF16)
    else:
        s_len = k_ref.shape[1]
        nblk = s_len // MOBA_BLOCK

        @pl.when(qi == 0)
        def _():
            kmh_sc[...] = jnp.zeros(kmh_sc.shape, BF16)
            kml_sc[...] = jnp.zeros(kml_sc.shape, BF16)
            for n in range(nblk):
                kb = k_ref[0, n * MOBA_BLOCK:(n + 1) * MOBA_BLOCK, :].astype(F32)
                km = jnp.sum(kb, axis=0, keepdims=True) * (1.0 / MOBA_BLOCK)
                hi = km.astype(BF16)
                kmh_sc[HEAD_DIM + n:HEAD_DIM + n + 1, :] = hi
                kml_sc[HEAD_DIM + n:HEAD_DIM + n + 1, :] = (km - hi.astype(F32)).astype(BF16)

        for hh in range(GROUP):
            q_sc[hh * tq:(hh + 1) * tq, :] = q_ref[0, :, hh * LANE:(hh + 1) * LANE]
        q0 = q_sc[...]
        gate = _nt_dot(q0, kmh_sc[...]) + _nt_dot(q0, kml_sc[...])
        lane_r = lax.broadcasted_iota(jnp.int32, (rows, LANE), 1)
        lane_f = lane_r.astype(F32)
        q_blk = (qi * tq) // MOBA_BLOCK
        elig = (lane_r >= HEAD_DIM) & (lane_r < HEAD_DIM + q_blk)
        gsel = jnp.where(elig, gate, NEG)
        pen = jnp.where((lane_r >= HEAD_DIM) & (lane_r < HEAD_DIM + nblk), NEG, 0.0)
        for _ in range(MOBA_TOPK):
            mx = jnp.max(gsel, axis=1, keepdims=True)
            idx = jnp.min(jnp.where(gsel == mx, lane_f, 1e9), axis=1, keepdims=True)
            pick = (lane_f == idx) & (mx > 0.5 * NEG)
            pen = jnp.where(pick, 0.0, pen)
            gsel = jnp.where(pick, NEG, gsel)
        pen = jnp.where(lane_r == HEAD_DIM + q_blk, 0.0, pen)
        q_sc[...] = (q0.astype(F32) + pen).astype(BF16)

    q = q_sc[...]
    _init_flash(m_sc, l_sc, acc_sc)

    def keys(start, width):
        k = k_ref[0, pl.ds(start, width), :]
        if kind == "moba":
            lane_k = lax.broadcasted_iota(jnp.int32, (width, LANE), 1)
            k = jnp.where(lane_k == HEAD_DIM + start // MOBA_BLOCK, 1.0, k.astype(F32)).astype(BF16)
        return k

    def step(start, width, bias=None):
        _flash_step(q, keys(start, width), v_ref[0, pl.ds(start, width), :], m_sc, l_sc, acc_sc, bias)

    def bias_rows(t):
        bt = bias_ref[t, 0]
        return bt if nmap == 1 else jnp.concatenate([bt, bt], axis=0)

    n_plain = jnp.maximum(qi - 1, 0) * tq
    n_far = n_plain // far_w

    def far(j, carry):
        step(pl.multiple_of(j * far_w, far_w), far_w)
        return carry

    lax.fori_loop(0, n_far, far, 0)

    def rest(j, carry):
        step(pl.multiple_of(n_far * far_w + j * tq, tq), tq)
        return carry

    lax.fori_loop(0, (n_plain - n_far * far_w) // tq, rest, 0)

    @pl.when(qi >= 1)
    def _():
        step(pl.multiple_of((qi - 1) * tq, tq), tq, bias_rows(1))

    step(pl.multiple_of(qi * tq, tq), tq, bias_rows(0))

    on = acc_sc[...] / l_sc[...]
    heads = []
    if kind == "diff":
        lam = _lambda_value(lamv_ref[...], lam_init)
        for hh in range(GROUP):
            a = on[hh * tq:(hh + 1) * tq] - lam * on[(GROUP + hh) * tq:(GROUP + hh + 1) * tq]
            ms = jnp.sum(a * a, axis=1, keepdims=True) * (1.0 / HEAD_DIM)
            heads.append(a * lax.rsqrt(ms + EPS) * gh_ref[...] * (1.0 - lam_init))
    else:
        heads = [on[hh * tq:(hh + 1) * tq] for hh in range(GROUP)]
    for pr in range(GROUP // 2):
        left, right = heads[2 * pr], pltpu.roll(heads[2 * pr + 1], HEAD_DIM, axis=1)
        o_ref[0, :, pr * LANE:(pr + 1) * LANE] = jnp.where(lane_t < HEAD_DIM, left, right).astype(o_ref.dtype)


def _gqa_prompt_attn(slots, bias, b, s, kind, lamv=None, gh=None, lam_init=0.0, tq=128, far_w=256):
    far_w = min(far_w, s)
    nslot = slots.shape[1] // LANE
    slots = slots.reshape(b, s, nslot * LANE)
    nmap = 2 if kind == "diff" else 1
    rows = nmap * GROUP * tq
    args = [slots, slots, slots, bias]
    specs = [pl.BlockSpec((1, tq, GROUP * LANE), lambda bi, g, qi: (bi, qi, g)),
             pl.BlockSpec((1, s, LANE), lambda bi, g, qi: (bi, 0, N_HEADS + g)),
             pl.BlockSpec((1, s, LANE), lambda bi, g, qi: (bi, 0, N_HEADS + N_KV + g)),
             pl.BlockSpec((2, 1, GROUP * tq, LANE), lambda bi, g, qi: (0, g, 0, 0))]
    scratch = [pltpu.VMEM((rows, LANE), BF16)] + [pltpu.VMEM((rows, LANE), F32)] * 3
    if kind == "diff":
        args += [lamv, gh]
        specs += [pl.BlockSpec((4, LANE), lambda bi, g, qi: (0, 0)), pl.BlockSpec((1, LANE), lambda bi, g, qi: (0, 0))]
    else:
        assert s % MOBA_BLOCK == 0 and s // MOBA_BLOCK <= 32
        scratch += [pltpu.VMEM((LANE, LANE), BF16)] * 2
    return pl.pallas_call(
        functools.partial(_gqa_prompt_body, kind=kind, tq=tq, far_w=far_w, lam_init=lam_init),
        grid=(b, N_KV, s // tq),
        in_specs=specs,
        out_specs=pl.BlockSpec((1, tq, GROUP * HEAD_DIM), lambda bi, g, qi: (bi, qi, g)),
        out_shape=jax.ShapeDtypeStruct((b, s, N_HEADS * HEAD_DIM), BF16),
        scratch_shapes=scratch,
        compiler_params=_cparams(("arbitrary", "arbitrary", "arbitrary"), VMEM_LIMIT),
        name=kind + "_prompt_attn",
    )(*args)


def _swa_prompt_body(sink_ref, q_ref, kp_ref, kc_ref, vp_ref, vc_ref, bias_ref, o_ref, *, tq):
    g = pl.program_id(1)
    qi = pl.program_id(2)
    q = jnp.concatenate([q_ref[0, :, hh * LANE:(hh + 1) * LANE] for hh in range(GROUP)], axis=0)
    sink = jnp.concatenate([jnp.full((tq, LANE), sink_ref[g * GROUP + hh], F32) for hh in range(GROUP)], axis=0)
    s_prev = _nt_dot(q, kp_ref[0]) + bias_ref[1, 0]
    s_prev = jnp.where(qi == 0, NEG, s_prev)
    s_cur = _nt_dot(q, kc_ref[0]) + bias_ref[0, 0]
    m = jnp.maximum(jnp.max(jnp.maximum(s_prev, s_cur), axis=1, keepdims=True), sink)
    p_prev = jnp.exp(s_prev - m)
    p_cur = jnp.exp(s_cur - m)
    den = jnp.sum(p_prev + p_cur, axis=1, keepdims=True) + jnp.exp(sink - m)
    o = (jnp.dot(p_prev.astype(BF16), vp_ref[0], preferred_element_type=F32)
         + jnp.dot(p_cur.astype(BF16), vc_ref[0], preferred_element_type=F32)) / den
    lane_t = lax.broadcasted_iota(jnp.int32, (tq, LANE), 1)
    for pr in range(GROUP // 2):
        left = o[2 * pr * tq:(2 * pr + 1) * tq]
        right = pltpu.roll(o[(2 * pr + 1) * tq:(2 * pr + 2) * tq], HEAD_DIM, axis=1)
        o_ref[0, :, pr * LANE:(pr + 1) * LANE] = jnp.where(lane_t < HEAD_DIM, left, right).astype(o_ref.dtype)


def _swa_prompt_attn(slots, bias, sinks, b, s, tq=128):
    nslot = slots.shape[1] // LANE
    slots = slots.reshape(b, s, nslot * LANE)
    prev = lambda off: (lambda bi, g, qi: (bi, jnp.maximum(qi - 1, 0), off + g))
    cur = lambda off: (lambda bi, g, qi: (bi, qi, off + g))
    return pl.pallas_call(
        functools.partial(_swa_prompt_body, tq=tq),
        grid=(b, N_KV, s // tq),
        in_specs=[pl.BlockSpec(memory_space=pltpu.SMEM),
                  pl.BlockSpec((1, tq, GROUP * LANE), lambda bi, g, qi: (bi, qi, g)),
                  pl.BlockSpec((1, tq, LANE), prev(N_HEADS)), pl.BlockSpec((1, tq, LANE), cur(N_HEADS)),
                  pl.BlockSpec((1, tq, LANE), prev(N_HEADS + N_KV)), pl.BlockSpec((1, tq, LANE), cur(N_HEADS + N_KV)),
                  pl.BlockSpec((2, 1, GROUP * tq, LANE), lambda bi, g, qi: (0, g, 0, 0))],
        out_specs=pl.BlockSpec((1, tq, GROUP * HEAD_DIM), lambda bi, g, qi: (bi, qi, g)),
        out_shape=jax.ShapeDtypeStruct((b, s, N_HEADS * HEAD_DIM), BF16),
        compiler_params=_cparams(("arbitrary", "arbitrary", "arbitrary")),
        name="swa_prompt_attn",
    )(sinks.astype(F32), slots, slots, slots, slots, slots, bias)


def _silu(x):
    return x / (1.0 + jnp.exp(-x))


def _ffn_prompt_body(x_ref, g_ref, wg_ref, wu_ref, cwg_ref, cwu_ref, cbg_ref, cbu_ref, wd_ref,
                     o_ref, sg_ref, su_ref, xn_sc, acc_sc, carry_sc, *, tiles_per_seq):
    i = pl.program_id(0)
    j = pl.program_id(1)
    tm = x_ref.shape[0]

    @pl.when(j == 0)
    def _():
        xn_sc[...] = _rms(x_ref[...], g_ref[...]).astype(BF16)
        acc_sc[...] = jnp.zeros(acc_sc.shape, F32)

    xb = xn_sc[...]
    first = (i % tiles_per_seq) == 0
    row = lax.broadcasted_iota(jnp.int32, (tm, wg_ref.shape[1]), 0)

    @pl.when(first)
    def _():
        carry_sc[j] = jnp.zeros(carry_sc.shape[1:], F32)

    def conv(h, cw_ref, cb_ref, idx):
        prev = carry_sc[j, idx]
        p6, p7 = prev[6:7], prev[7:8]
        h1 = jnp.where(row == 0, p7, pltpu.roll(h, 1, axis=0))
        h2 = jnp.where(row == 0, p6, jnp.where(row == 1, p7, pltpu.roll(h, 2, axis=0)))
        carry_sc[j, idx] = h[tm - 8:]
        return cb_ref[...] + cw_ref[0:1] * h2 + cw_ref[1:2] * h1 + cw_ref[2:3] * h

    hg = jnp.dot(xb, wg_ref[...], preferred_element_type=F32)
    hu = jnp.dot(xb, wu_ref[...], preferred_element_type=F32)
    sg_ref[0] = hg[tm - 8:]
    su_ref[0] = hu[tm - 8:]
    act = _silu(conv(hg, cwg_ref, cbg_ref, 0)) * conv(hu, cwu_ref, cbu_ref, 1)
    acc_sc[...] += jnp.dot(act.astype(BF16), wd_ref[...], preferred_element_type=F32)

    @pl.when(j == pl.num_programs(1) - 1)
    def _():
        o_ref[...] = x_ref[...] + acc_sc[...]


def _ffn_prompt(x, g, w_up, conv_w, conv_b, w_down, b, s, tm=1024, fc=256):
    t, d = x.shape
    dff = w_down.shape[0]
    tm = min(tm, s)
    nf = dff // fc
    row = lambda i, j: (i, 0)
    gcol = lambda i, j: (0, j)
    ucol = lambda i, j: (0, nf + j)
    tps = s // tm
    out, sg, su = pl.pallas_call(
        functools.partial(_ffn_prompt_body, tiles_per_seq=tps),
        grid=(t // tm, nf),
        in_specs=[pl.BlockSpec((tm, d), row), pl.BlockSpec((1, d), lambda i, j: (0, 0)),
                  pl.BlockSpec((d, fc), gcol), pl.BlockSpec((d, fc), ucol),
                  pl.BlockSpec((3, fc), gcol), pl.BlockSpec((3, fc), ucol),
                  pl.BlockSpec((1, fc), gcol), pl.BlockSpec((1, fc), ucol),
                  pl.BlockSpec((fc, d), lambda i, j: (j, 0))],
        out_specs=[pl.BlockSpec((tm, d), row),
                   pl.BlockSpec((1, 8, fc), lambda i, j: (i, 0, j)),
                   pl.BlockSpec((1, 8, fc), lambda i, j: (i, 0, j))],
        out_shape=[jax.ShapeDtypeStruct((t, d), F32), jax.ShapeDtypeStruct((t // tm, 8, dff), F32),
                   jax.ShapeDtypeStruct((t // tm, 8, dff), F32)],
        scratch_shapes=[pltpu.VMEM((tm, d), BF16), pltpu.VMEM((tm, d), F32), pltpu.VMEM((nf, 2, 8, fc), F32)],
        compiler_params=_cparams(("arbitrary", "arbitrary"), VMEM_LIMIT),
        name="ffn_prompt",
    )(x, g.reshape(1, d), w_up, w_up, conv_w, conv_w, conv_b.reshape(1, -1), conv_b.reshape(1, -1), w_down)
    last = slice(tps - 1, None, tps)
    state = jnp.concatenate([sg[last, 6:8], su[last, 6:8]], axis=-1)
    return out, state


def _ffn_sample_body(x_ref, g_ref, wg_ref, wu_ref, cwg_ref, cwu_ref, cbg_ref, cbu_ref, wd_ref,
                     s0g_ref, s1g_ref, s0u_ref, s1u_ref, o_ref, hg_ref, hu_ref, xn_sc, acc_sc):
    j = pl.program_id(0)

    @pl.when(j == 0)
    def _():
        xn_sc[...] = _rms(x_ref[...], g_ref[...]).astype(BF16)
        acc_sc[...] = jnp.zeros(acc_sc.shape, F32)

    xb = xn_sc[...]
    hg = jnp.dot(xb, wg_ref[...], preferred_element_type=F32)
    hu = jnp.dot(xb, wu_ref[...], preferred_element_type=F32)
    hg_ref[...] = hg
    hu_ref[...] = hu
    cg = cbg_ref[...] + cwg_ref[0:1] * s0g_ref[...] + cwg_ref[1:2] * s1g_ref[...] + cwg_ref[2:3] * hg
    cu = cbu_ref[...] + cwu_ref[0:1] * s0u_ref[...] + cwu_ref[1:2] * s1u_ref[...] + cwu_ref[2:3] * hu
    acc_sc[...] += jnp.dot((_silu(cg) * cu).astype(BF16), wd_ref[...], preferred_element_type=F32)

    @pl.when(j == pl.num_programs(0) - 1)
    def _():
        o_ref[...] = x_ref[...] + acc_sc[...]


def _ffn_sample(x, g, w_up, conv_w, conv_b, w_down, state, fc=256):
    t, d = x.shape
    dff = w_down.shape[0]
    nf = dff // fc
    s0, s1 = state[:, 0], state[:, 1]
    full = lambda j: (0, 0)
    gcol = lambda j: (0, j)
    ucol = lambda j: (0, nf + j)
    out, hg, hu = pl.pallas_call(
        _ffn_sample_body,
        grid=(nf,),
        in_specs=[pl.BlockSpec((t, d), full), pl.BlockSpec((1, d), full),
                  pl.BlockSpec((d, fc), gcol), pl.BlockSpec((d, fc), ucol),
                  pl.BlockSpec((3, fc), gcol), pl.BlockSpec((3, fc), ucol),
                  pl.BlockSpec((1, fc), gcol), pl.BlockSpec((1, fc), ucol),
                  pl.BlockSpec((fc, d), lambda j: (j, 0)),
                  pl.BlockSpec((t, fc), gcol), pl.BlockSpec((t, fc), gcol),
                  pl.BlockSpec((t, fc), ucol), pl.BlockSpec((t, fc), ucol)],
        out_specs=[pl.BlockSpec((t, d), full), pl.BlockSpec((t, fc), gcol), pl.BlockSpec((t, fc), gcol)],
        out_shape=[jax.ShapeDtypeStruct((t, d), F32), jax.ShapeDtypeStruct((t, dff), F32),
                   jax.ShapeDtypeStruct((t, dff), F32)],
        scratch_shapes=[pltpu.VMEM((t, d), BF16), pltpu.VMEM((t, d), F32)],
        compiler_params=_cparams(("arbitrary",), VMEM_LIMIT),
        name="ffn_sample",
    )(x, g.reshape(1, d), w_up, w_up, conv_w, conv_w, conv_b.reshape(1, -1), conv_b.reshape(1, -1), w_down,
      s0, s1, s0, s1)
    h = jnp.concatenate([hg, hu], axis=-1)
    return out, jnp.stack([s1, h], axis=1)


def _page_copies(pt_ref, seq, slot, n_pages, page_off, pairs):
    def make(p, cache, buf, sem):
        pg = pt_ref[seq, p] + page_off
        return pltpu.make_async_copy(cache.at[pg], buf.at[slot, pl.ds(pl.multiple_of(p * PAGE, PAGE), PAGE)], sem.at[slot])

    def start(p, c):
        for cache, buf, sem in pairs:
            make(p, cache, buf, sem).start()
        return c

    def wait(p, c):
        for cache, buf, sem in pairs:
            make(p, cache, buf, sem).wait()
        return c

    return (lambda: lax.fori_loop(0, n_pages, start, 0)), (lambda: lax.fori_loop(0, n_pages, wait, 0))


def _prefetch_pages(pt_ref, n_pages, page_off, pairs):
    b = pl.program_id(0)
    nb = pl.num_programs(0)

    @pl.when(b == 0)
    def _():
        _page_copies(pt_ref, 0, 0, n_pages, page_off, pairs)[0]()

    @pl.when(b + 1 < nb)
    def _():
        _page_copies(pt_ref, b + 1, (b + 1) % 2, n_pages, page_off, pairs)[0]()

    slot = b % 2
    _page_copies(pt_ref, b, slot, n_pages, page_off, pairs)[1]()
    return slot


def _gqa_decode_body(pt_ref, q_ref, knew_ref, vnew_ref, bdec_ref, bself_ref, *rest,
                     kind, n_pages, page_off, chunk, lam_init):
    if kind == "diff":
        lamv_ref, gh_ref, kc_hbm, vc_hbm, o_ref, kbuf, vbuf, s_sc, p_sc, ksem, vsem = rest
    else:
        kc_hbm, vc_hbm, o_ref, kbuf, vbuf, s_sc, p_sc, km_sc, ksem, vsem = rest
    slot = _prefetch_pages(pt_ref, n_pages, page_off, [(kc_hbm, kbuf, ksem), (vc_hbm, vbuf, vsem)])
    length = n_pages * PAGE
    q = q_ref[0]
    rows = q.shape[0]
    kw = q.shape[1]

    if kind == "moba":
        nblk = length // MOBA_BLOCK
        km_sc[...] = jnp.zeros(km_sc.shape, F32)
        for n in range(nblk):
            kb = kbuf[slot, n * MOBA_BLOCK:(n + 1) * MOBA_BLOCK, :]
            km_sc[n:n + 1, :] = jnp.sum(kb, axis=0, keepdims=True) * (1.0 / MOBA_BLOCK)
        km = km_sc[...]
        hi = km.astype(BF16)
        lo = (km - hi.astype(F32)).astype(BF16)
        gate = _nt_dot(q, hi) + _nt_dot(q, lo)
        lane_r = lax.broadcasted_iota(jnp.int32, (rows, LANE), 1)
        lane_f = lane_r.astype(F32)
        gsel = jnp.where(lane_r < nblk, gate, NEG)
        pen = jnp.where(lane_r < nblk, NEG, 0.0)
        for _ in range(MOBA_TOPK):
            mx = jnp.max(gsel, axis=1, keepdims=True)
            idx = jnp.min(jnp.where(gsel == mx, lane_f, 1e9), axis=1, keepdims=True)
            pick = (lane_f == idx) & (mx > 0.5 * NEG)
            pen = jnp.where(pick, 0.0, pen)
            gsel = jnp.where(pick, NEG, gsel)
        for n in range(nblk):
            kb = kbuf[slot, n * MOBA_BLOCK:(n + 1) * MOBA_BLOCK, :].astype(BF16)
            pen_n = jnp.sum(jnp.where(lane_r == n, pen, 0.0), axis=1, keepdims=True)
            s_sc[:, n * MOBA_BLOCK:(n + 1) * MOBA_BLOCK] = _nt_dot(q, kb) + pen_n
    else:
        for c in range(length // chunk):
            kb = kbuf[slot, c * chunk:(c + 1) * chunk, :].astype(BF16)
            s_sc[:, c * chunk:(c + 1) * chunk] = _nt_dot(q, kb)

    s_sc[:, length - LANE:] = s_sc[:, length - LANE:] + bdec_ref[...]
    s_self = jnp.sum(q.astype(F32) * knew_ref[0], axis=1, keepdims=True) + bself_ref[:, 0:1]
    s = s_sc[...]
    m = jnp.maximum(jnp.max(s, axis=1, keepdims=True), s_self)
    p = jnp.exp(s - m)
    p_self = jnp.exp(s_self - m)
    den = jnp.sum(p, axis=1, keepdims=True) + p_self
    p_sc[...] = p.astype(BF16)
    o = p_self * vnew_ref[0]
    for c in range(length // chunk):
        vb = vbuf[slot, c * chunk:(c + 1) * chunk, :].astype(BF16)
        o = o + jnp.dot(p_sc[:, c * chunk:(c + 1) * chunk], vb, preferred_element_type=F32)
    on = o / den
    if kind == "diff":
        lam = _lambda_value(lamv_ref[...], lam_init)
        a = on[:N_HEADS] - lam * on[N_HEADS:]
        r16 = lax.broadcasted_iota(jnp.int32, (N_HEADS, kw), 0)
        l16 = lax.broadcasted_iota(jnp.int32, (N_HEADS, kw), 1)
        own = jnp.right_shift(l16, 6) == jnp.right_shift(r16, 2)
        ms = jnp.sum(jnp.where(own, a * a, 0.0), axis=1, keepdims=True) * (1.0 / HEAD_DIM)
        on = a * lax.rsqrt(ms + EPS) * gh_ref[...] * (1.0 - lam_init)
    o_ref[0] = on


def _gqa_decode_attn(page_table, qdec, k_new, v_new, bdec, bself, k_cache, v_cache, layer, kind,
                     lamv=None, gh=None, lam_init=0.0, chunk=1024):
    db, n_pages = page_table.shape
    length = n_pages * PAGE
    chunk = min(chunk, length)
    rows = qdec.shape[1] // (N_KV * HEAD_DIM)
    kw = N_KV * HEAD_DIM
    n_pool = k_cache.shape[1]
    kc = k_cache.reshape(k_cache.shape[0] * n_pool, PAGE, kw)
    vc = v_cache.reshape(v_cache.shape[0] * n_pool, PAGE, kw)
    qdec = qdec.reshape(db, rows, kw)
    args = [qdec, k_new.reshape(db, 1, kw), v_new.reshape(db, 1, kw), bdec, bself]
    specs = [pl.BlockSpec((1, rows, kw), lambda b, pt: (b, 0, 0)),
             pl.BlockSpec((1, 1, kw), lambda b, pt: (b, 0, 0)),
             pl.BlockSpec((1, 1, kw), lambda b, pt: (b, 0, 0)),
             pl.BlockSpec((rows, LANE), lambda b, pt: (0, 0)),
             pl.BlockSpec((rows, LANE), lambda b, pt: (0, 0))]
    scratch = [pltpu.VMEM((2, length, kw), F32), pltpu.VMEM((2, length, kw), F32),
               pltpu.VMEM((rows, length), F32), pltpu.VMEM((rows, length), BF16)]
    if kind == "diff":
        args += [lamv, gh]
        specs += [pl.BlockSpec((4, LANE), lambda b, pt: (0, 0)), pl.BlockSpec((1, kw), lambda b, pt: (0, 0))]
    else:
        assert length % MOBA_BLOCK == 0 and length // MOBA_BLOCK <= LANE
        scratch.append(pltpu.VMEM((LANE, kw), F32))
    scratch += [pltpu.SemaphoreType.DMA((2,)), pltpu.SemaphoreType.DMA((2,))]
    args += [kc, vc]
    specs += [pl.BlockSpec(memory_space=pl.ANY), pl.BlockSpec(memory_space=pl.ANY)]
    out_rows = N_HEADS
    return pl.pallas_call(
        functools.partial(_gqa_decode_body, kind=kind, n_pages=n_pages, page_off=layer * n_pool,
                          chunk=chunk, lam_init=lam_init),
        grid_spec=pltpu.PrefetchScalarGridSpec(
            num_scalar_prefetch=1, grid=(db,), in_specs=specs,
            out_specs=pl.BlockSpec((1, out_rows, kw), lambda b, pt: (b, 0, 0)),
            scratch_shapes=scratch),
        out_shape=jax.ShapeDtypeStruct((db, out_rows, kw), F32),
        compiler_params=_cparams(("arbitrary",), VMEM_LIMIT),
        name=kind + "_decode_attn",
    )(page_table, *args)


def _mla_decode_body(pt_ref, ql_ref, qp_ref, lnew_ref, pnew_ref, lat_hbm, kpe_hbm, o_ref,
                     lbuf, pbuf, s_sc, p_sc, lsem, psem, *, n_pages, page_off, chunk):
    slot = _prefetch_pages(pt_ref, n_pages, page_off, [(lat_hbm, lbuf, lsem), (kpe_hbm, pbuf, psem)])
    length = n_pages * PAGE
    ql = ql_ref[0]
    qp = qp_ref[0]
    for c in range(length // chunk):
        lb = lbuf[slot, c * chunk:(c + 1) * chunk, :].astype(BF16)
        pb = pbuf[slot, c * chunk:(c + 1) * chunk, :].astype(BF16)
        s_sc[:, c * chunk:(c + 1) * chunk] = _nt_dot(ql, lb) + _nt_dot(qp, pb)
    s_self = (jnp.sum(ql.astype(F32) * lnew_ref[0], axis=1, keepdims=True)
              + jnp.sum(qp.astype(F32) * pnew_ref[0], axis=1, keepdims=True))
    s = s_sc[...]
    m = jnp.maximum(jnp.max(s, axis=1, keepdims=True), s_self)
    p = jnp.exp(s - m)
    p_self = jnp.exp(s_self - m)
    den = jnp.sum(p, axis=1, keepdims=True) + p_self
    p_sc[...] = p.astype(BF16)
    o = p_self * lnew_ref[0]
    for c in range(length // chunk):
        lb = lbuf[slot, c * chunk:(c + 1) * chunk, :].astype(BF16)
        o = o + jnp.dot(p_sc[:, c * chunk:(c + 1) * chunk], lb, preferred_element_type=F32)
    o_ref[0] = o / den


def _mla_decode_attn(page_table, q_lat, q_pe, lat_new, kpe_new, lat_cache, kpe_cache, layer, chunk=1024):
    db, n_pages = page_table.shape
    length = n_pages * PAGE
    chunk = min(chunk, length)
    n_pool = lat_cache.shape[1]
    lc = lat_cache.reshape(lat_cache.shape[0] * n_pool, PAGE, MLA_KV_LORA)
    pc = kpe_cache.reshape(kpe_cache.shape[0] * n_pool, PAGE, MLA_ROPE)
    specs = [pl.BlockSpec((1, N_HEADS, MLA_KV_LORA), lambda b, pt: (b, 0, 0)),
             pl.BlockSpec((1, N_HEADS, MLA_ROPE), lambda b, pt: (b, 0, 0)),
             pl.BlockSpec((1, 1, MLA_KV_LORA), lambda b, pt: (b, 0, 0)),
             pl.BlockSpec((1, 1, MLA_ROPE), lambda b, pt: (b, 0, 0)),
             pl.BlockSpec(memory_space=pl.ANY), pl.BlockSpec(memory_space=pl.ANY)]
    scratch = [pltpu.VMEM((2, length, MLA_KV_LORA), F32), pltpu.VMEM((2, length, MLA_ROPE), F32),
               pltpu.VMEM((N_HEADS, length), F32), pltpu.VMEM((N_HEADS, length), BF16),
               pltpu.SemaphoreType.DMA((2,)), pltpu.SemaphoreType.DMA((2,))]
    return pl.pallas_call(
        functools.partial(_mla_decode_body, n_pages=n_pages, page_off=layer * n_pool, chunk=chunk),
        grid_spec=pltpu.PrefetchScalarGridSpec(
            num_scalar_prefetch=1, grid=(db,), in_specs=specs,
            out_specs=pl.BlockSpec((1, N_HEADS, MLA_KV_LORA), lambda b, pt: (b, 0, 0)),
            scratch_shapes=scratch),
        out_shape=jax.ShapeDtypeStruct((db, N_HEADS, MLA_KV_LORA), F32),
        compiler_params=_cparams(("arbitrary",), VMEM_LIMIT),
        name="mla_decode_attn",
    )(page_table, q_lat, q_pe, lat_new.reshape(db, 1, -1), kpe_new.reshape(db, 1, -1), lc, pc)


def _swa_decode_body(q_ref, k_ref, v_ref, knew_ref, vnew_ref, bdec_ref, bself_ref, sink_ref, o_ref, *, nseq):
    sink = sink_ref[:, 0:1]
    for i in range(nseq):
        q = q_ref[i]
        s = _nt_dot(q, k_ref[i].astype(BF16)) + bdec_ref[...]
        s_self = jnp.sum(q.astype(F32) * knew_ref[i], axis=1, keepdims=True) + bself_ref[:, 0:1]
        m = jnp.maximum(jnp.maximum(jnp.max(s, axis=1, keepdims=True), s_self), sink)
        p = jnp.exp(s - m)
        p_self = jnp.exp(s_self - m)
        den = jnp.sum(p, axis=1, keepdims=True) + p_self + jnp.exp(sink - m)
        o = jnp.dot(p.astype(BF16), v_ref[i].astype(BF16), preferred_element_type=F32) + p_self * vnew_ref[i]
        o_ref[i] = o / den


def _swa_decode_attn(qdec, k_buf, v_buf, k_new, v_new, bdec, bself, sink_rows, nseq=8):
    db, w = k_buf.shape[:2]
    assert w == LANE
    kw = N_KV * HEAD_DIM
    nseq = min(nseq, db)
    qdec = qdec.reshape(db, N_HEADS, kw)
    seq = lambda b: (b, 0, 0)
    full = lambda b: (0, 0)
    return pl.pallas_call(
        functools.partial(_swa_decode_body, nseq=nseq),
        grid=(db // nseq,),
        in_specs=[pl.BlockSpec((nseq, N_HEADS, kw), seq),
                  pl.BlockSpec((nseq, w, kw), seq), pl.BlockSpec((nseq, w, kw), seq),
                  pl.BlockSpec((nseq, 1, kw), seq), pl.BlockSpec((nseq, 1, kw), seq),
                  pl.BlockSpec((N_HEADS, LANE), full), pl.BlockSpec((N_HEADS, LANE), full),
                  pl.BlockSpec((N_HEADS, LANE), full)],
        out_specs=pl.BlockSpec((nseq, N_HEADS, kw), seq),
        out_shape=jax.ShapeDtypeStruct((db, N_HEADS, kw), F32),
        compiler_params=_cparams(("arbitrary",)),
        name="swa_decode_attn",
    )(qdec, k_buf.reshape(db, w, kw), v_buf.reshape(db, w, kw), k_new.reshape(db, 1, kw), v_new.reshape(db, 1, kw),
      bdec, bself, sink_rows)


def _rot_cols(w):
    half = w.shape[-1] // 2
    return jnp.concatenate([-w[..., half:], w[..., :half]], axis=-1)


def _pad_lanes(w, width=LANE):
    return jnp.pad(w, [(0, 0)] * (w.ndim - 1) + [(0, width - w.shape[-1])])


def _gqa_weights(w_qkv, scale):
    d = w_qkv.shape[0]
    nq = N_HEADS * HEAD_DIM
    nk = N_KV * HEAD_DIM
    wq = w_qkv[:, :nq].reshape(d, N_HEADS, HEAD_DIM) * scale
    wk = w_qkv[:, nq:nq + nk].reshape(d, N_KV, HEAD_DIM)
    wv = w_qkv[:, nq + nk:].reshape(d, N_KV, HEAD_DIM)
    slots = jnp.concatenate([_pad_lanes(wq), _pad_lanes(wk), _pad_lanes(wv)], axis=1)
    return slots.reshape(d, -1).astype(BF16), w_qkv[:, nq:].astype(BF16), wq


def _decode_q_weights(wq, nmap):
    d = wq.shape[0]
    kw = N_KV * HEAD_DIM
    out = jnp.zeros((d, nmap, N_HEADS, kw), wq.dtype)
    for h in range(N_HEADS):
        base = (h // GROUP) * HEAD_DIM
        if nmap == 1:
            out = out.at[:, 0, h, base:base + HEAD_DIM].set(wq[:, h])
        else:
            for mp in range(2):
                out = out.at[:, mp, h, base + mp * DIFF_QK:base + (mp + 1) * DIFF_QK].set(
                    wq[:, h, mp * DIFF_QK:(mp + 1) * DIFF_QK])
    return out.reshape(d, -1).astype(BF16)


def _decode_o_weights(w_o):
    d = w_o.shape[1]
    kw = N_KV * HEAD_DIM
    out = jnp.zeros((N_HEADS, kw, d), w_o.dtype)
    for h in range(N_HEADS):
        base = (h // GROUP) * HEAD_DIM
        out = out.at[h, base:base + HEAD_DIM].set(w_o[h * HEAD_DIM:(h + 1) * HEAD_DIM])
    return out.reshape(N_HEADS * kw, d).astype(BF16)


def _rope_tables(pos):
    half = MLA_ROPE // 2
    inv = ROPE_THETA ** (-jnp.arange(half, dtype=F32) / half)
    ang = pos.astype(F32)[:, None] * inv[None, :]
    cos = jnp.concatenate([jnp.cos(ang)] * 2, axis=1)
    sin = jnp.concatenate([jnp.sin(ang)] * 2, axis=1)
    n = pos.shape[0]
    z64, z32, o64 = jnp.zeros((n, 64), F32), jnp.zeros((n, 32), F32), jnp.ones((n, 64), F32)
    cosq = jnp.concatenate([o64, cos, z32], axis=1)
    sinq = jnp.concatenate([z64, sin, z32], axis=1)
    cosk = jnp.concatenate([z64, cos, z32], axis=1)
    return cosq, sinq, cosk


def kernel(x_prompt, x_sample, cache_mla_latent, cache_mla_krope, cache_diff_k, cache_diff_v, cache_moba_k, cache_moba_v, state_swa_k, state_swa_v, state_ffn_conv, page_table, rel_bias, norm_mix_g, norm_ffn_g, norm_final_g, mla_w_dq, mla_g_q, mla_w_uq, mla_w_dkv, mla_g_kv, mla_w_uk, mla_w_uv, mla_w_o, diff_w_qkv, diff_lambda, diff_g_head, diff_w_o, moba_w_qkv, moba_w_o, swa_w_qkv, swa_sinks, swa_w_o, ffn_w_up, ffn_conv_w, ffn_conv_b, ffn_w_down):
    b, s, d = x_prompt.shape
    db = x_sample.shape[0]
    assert x_sample.shape[1] == 1
    depth = norm_mix_g.shape[0]
    past_len = page_table.shape[1] * PAGE
    kw = N_KV * HEAD_DIM
    np_pages = s // PAGE

    tiles = _bias_tiles(rel_bias)
    bias_causal = tiles[T_DIAG:T_PREV + 1].reshape(2, N_KV, GROUP * LANE, LANE)
    bias_swa = tiles[T_SWA_DIAG:T_SWA_PREV + 1].reshape(2, N_KV, GROUP * LANE, LANE)
    bdec = tiles[T_DEC, :, 0, :]
    bself = jnp.broadcast_to(tiles[T_DIAG, :, 0, 0:1], (N_HEADS, LANE))
    bdec_swa = tiles[T_SWA_DEC, :, 0, :]
    bself_swa = jnp.broadcast_to(tiles[T_SWA_DIAG, :, 0, 0:1], (N_HEADS, LANE))

    cos_p = _rope_tables(jnp.arange(s, dtype=jnp.int32))
    cos_s = tuple(jnp.broadcast_to(t, (db, LANE)) for t in _rope_tables(jnp.full((1,), past_len, jnp.int32)))

    hp = x_prompt.reshape(b * s, d)
    hs = x_sample.reshape(db, d)
    outs = {k: [] for k in ("lat_p", "lat_s", "kpe_p", "kpe_s", "dk_p", "dk_s", "dv_p", "dv_s", "mk_p", "mk_s",
                            "mv_p", "mv_s", "sk_p", "sk_s", "sv_p", "sv_s", "conv_p", "conv_s")}

    for i in range(depth):
        kind, l = i % 4, i // 4
        g_mix = norm_mix_g[i]
        if kind == 0:
            w_pe = mla_w_dkv[l][:, MLA_KV_LORA:]
            w1 = jnp.concatenate([mla_w_dq[l], mla_w_dkv[l][:, :MLA_KV_LORA], w_pe, _rot_cols(w_pe),
                                  jnp.zeros((d, LANE - 2 * MLA_ROPE), F32)], axis=1).astype(BF16)
            wuq = mla_w_uq[l] * MLA_SCALE
            wuq = jnp.concatenate([wuq, _rot_cols(wuq[..., MLA_NOPE:])], axis=-1).reshape(MLA_Q_LORA, -1).astype(BF16)
            wuk = _pad_lanes(mla_w_uk[l]).reshape(MLA_KV_LORA, -1).astype(BF16)
            wuv = mla_w_uv[l].reshape(MLA_KV_LORA, -1).astype(BF16)
            w_o = mla_w_o[l].astype(BF16)
            (s1,) = _proj(hp, [w1], [F32], g=g_mix, name="mla_down_p")
            qs, ks, v, lat, kpe = _mla_stage2(s1, mla_g_q[l], mla_g_kv[l], *cos_p, wuq, wuk, wuv)
            o = _mla_prompt_attn(qs, ks, v, b, s)
            (hp,) = _proj(o.reshape(b * s, -1), [w_o], [F32], res=hp, name="mla_out_p")
            outs["lat_p"].append(lat.reshape(b, np_pages, PAGE, MLA_KV_LORA))
            outs["kpe_p"].append(kpe[:, HEAD_DIM:HEAD_DIM + MLA_ROPE].reshape(b, np_pages, PAGE, MLA_ROPE))
            (s1,) = _proj(hs, [w1], [F32], g=g_mix, name="mla_down_s")
            qs, _, _, lat, kpe = _mla_stage2(s1, mla_g_q[l], mla_g_kv[l], *cos_s, wuq, wuk, wuv)
            kpe = kpe[:, HEAD_DIM:HEAD_DIM + MLA_ROPE]
            wuk_t = jnp.pad(jnp.transpose(mla_w_uk[l], (1, 2, 0)),
                            ((0, 0), (0, LANE - MLA_NOPE), (0, 0))).astype(BF16)
            q_lat = _headwise(qs, wuk_t, BF16, "mla_q_absorb").reshape(db, N_HEADS, MLA_KV_LORA)
            q_pe = qs.reshape(db, N_HEADS, LANE)[:, :, HEAD_DIM:HEAD_DIM + MLA_ROPE]
            o_lat = _mla_decode_attn(page_table, q_lat, q_pe, lat, kpe, cache_mla_latent, cache_mla_krope, l)
            wuv_h = mla_w_uv[l].transpose(1, 0, 2)
            wuv_pair = jnp.concatenate(
                [jnp.concatenate([wuv_h[0::2], jnp.zeros_like(wuv_h[0::2])], axis=-1),
                 jnp.concatenate([jnp.zeros_like(wuv_h[1::2]), wuv_h[1::2]], axis=-1)], axis=1).astype(BF16)
            o = _headwise(o_lat.reshape(db, -1), wuv_pair, BF16, "mla_v_absorb")
            (hs,) = _proj(o, [w_o], [F32], res=hs, name="mla_out_s")
            outs["lat_s"].append(lat.reshape(db, 1, MLA_KV_LORA))
            outs["kpe_s"].append(kpe.reshape(db, 1, MLA_ROPE))
        else:
            w_qkv, w_o, scale, nmap = {
                1: (diff_w_qkv, diff_w_o, DIFF_SCALE, 2),
                2: (moba_w_qkv, moba_w_o, ATTN_SCALE, 1),
                3: (swa_w_qkv, swa_w_o, ATTN_SCALE, 1)}[kind]
            w_slots, w_kv, wq = _gqa_weights(w_qkv[l], scale)
            w_qdec = _decode_q_weights(wq, nmap)
            w_ob = w_o[l].astype(BF16)
            w_odec = _decode_o_weights(w_o[l])
            slots, kv = _proj(hp, [w_slots, w_kv], [BF16, F32], g=g_mix, name="qkv_p%d" % kind)
            qdec, kv_s = _proj(hs, [w_qdec, w_kv], [BF16, F32], g=g_mix, name="qkv_s%d" % kind)
            k_p = kv[:, :kw].reshape(b, s, N_KV, HEAD_DIM)
            v_p = kv[:, kw:].reshape(b, s, N_KV, HEAD_DIM)
            k_s, v_s = kv_s[:, :kw], kv_s[:, kw:]
            if kind == 1:
                lam_init = 0.8 - 0.6 * math.exp(-0.3 * i)
                lamv = _pad_lanes(diff_lambda[l].astype(F32))
                gh = _pad_lanes(diff_g_head[l].astype(F32).reshape(1, -1))
                o = _gqa_prompt_attn(slots, bias_causal, b, s, "diff", lamv, gh, lam_init)
                gh4 = jnp.tile(diff_g_head[l].astype(F32).reshape(1, -1), (1, N_KV))
                o_s = _gqa_decode_attn(page_table, qdec, k_s, v_s, jnp.concatenate([bdec, bdec], axis=0),
                                       jnp.concatenate([bself, bself], axis=0), cache_diff_k, cache_diff_v, l,
                                       "diff", lamv, gh4, lam_init)
                pk, pv, sk, sv = "dk_p", "dv_p", "dk_s", "dv_s"
            elif kind == 2:
                o = _gqa_prompt_attn(slots, bias_causal, b, s, "moba")
                o_s = _gqa_decode_attn(page_table, qdec, k_s, v_s, bdec, bself, cache_moba_k, cache_moba_v, l, "moba")
                pk, pv, sk, sv = "mk_p", "mv_p", "mk_s", "mv_s"
            else:
                o = _swa_prompt_attn(slots, bias_swa, swa_sinks[l], b, s)
                sink_rows = jnp.broadcast_to(swa_sinks[l].astype(F32)[:, None], (N_HEADS, LANE))
                o_s = _swa_decode_attn(qdec, state_swa_k[l], state_swa_v[l], k_s, v_s, bdec_swa, bself_swa, sink_rows)
            (hp,) = _proj(o.reshape(b * s, -1), [w_ob], [F32], res=hp, name="attn_out_p%d" % kind)
            (hs,) = _proj(o_s.reshape(db, -1), [w_odec], [F32], res=hs, name="attn_out_s%d" % kind)
            if kind == 3:
                w_buf = state_swa_k.shape[2]
                outs["sk_p"].append(k_p[:, s - w_buf:])
                outs["sv_p"].append(v_p[:, s - w_buf:])
                outs["sk_s"].append(jnp.concatenate([state_swa_k[l], k_s.reshape(db, 1, N_KV, HEAD_DIM)], axis=1)[:, 1:])
                outs["sv_s"].append(jnp.concatenate([state_swa_v[l], v_s.reshape(db, 1, N_KV, HEAD_DIM)], axis=1)[:, 1:])
            else:
                outs[pk].append(k_p.reshape(b, np_pages, PAGE, N_KV, HEAD_DIM))
                outs[pv].append(v_p.reshape(b, np_pages, PAGE, N_KV, HEAD_DIM))
                outs[sk].append(k_s.reshape(db, 1, N_KV, HEAD_DIM))
                outs[sv].append(v_s.reshape(db, 1, N_KV, HEAD_DIM))

        w_up = ffn_w_up[i].astype(BF16)
        w_down = ffn_w_down[i].astype(BF16)
        hp, conv_p = _ffn_prompt(hp, norm_ffn_g[i], w_up, ffn_conv_w[i], ffn_conv_b[i], w_down, b, s)
        hs, conv_s = _ffn_sample(hs, norm_ffn_g[i], w_up, ffn_conv_w[i], ffn_conv_b[i], w_down, state_ffn_conv[i])
        outs["conv_p"].append(conv_p)
        outs["conv_s"].append(conv_s)

    y_prompt = _final_norm(hp, norm_final_g).reshape(b, s, d)
    y_sample = _final_norm(hs, norm_final_g).reshape(db, 1, d)
    st = lambda k: jnp.stack(outs[k])
    return (y_prompt, y_sample,
            st("lat_p"), st("lat_s"), st("kpe_p"), st("kpe_s"),
            st("dk_p"), st("dk_s"), st("dv_p"), st("dv_s"),
            st("mk_p"), st("mk_s"), st("mv_p"), st("mv_s"),
            st("sk_p"), st("sk_s"), st("sv_p"), st("sv_s"),
            st("conv_p"), st("conv_s"))
```

```python
import functools
import math

import jax
import jax.numpy as jnp
import numpy as np
from jax import lax
from jax.experimental import pallas as pl
from jax.experimental.pallas import tpu as pltpu

F32 = jnp.float32
BF16 = jnp.bfloat16

LANE = 128
HEAD_DIM = 64
N_HEADS = 16
N_KV = 4
GROUP = 4
PAGE = 128
MLA_Q_LORA = 384
MLA_KV_LORA = 256
MLA_NOPE = 64
MLA_ROPE = 32
LOG2E = 1.4426950408889634
MLA_SCALE = (MLA_NOPE + MLA_ROPE) ** -0.5
ATTN_SCALE = HEAD_DIM ** -0.5
DIFF_QK = 32
DIFF_SCALE = DIFF_QK ** -0.5
ROPE_THETA = 10000.0
MOBA_BLOCK = 256
MOBA_TOPK = 3
N_BUCKETS = 32
EPS = 1e-6
NEG = -1e30
ONES_LANE = HEAD_DIM
CHUNK = 512
PAGE_UNROLL = 8
VMEM_LIMIT = 48 * 1024 * 1024

T_DIAG, T_PREV, T_SWA_DIAG, T_SWA_PREV, T_DEC, T_SWA_DEC = range(6)
_SHIFTED_TILES = (T_DIAG, T_PREV, T_DEC)


def _cparams(sem, vmem=None):
    return pltpu.CompilerParams(dimension_semantics=sem, vmem_limit_bytes=vmem)


def _nt_dot(a, b):
    return lax.dot_general(a, b, (((1,), (1,)), ((), ())), preferred_element_type=F32)


def _rms(x, g):
    return x * lax.rsqrt(jnp.mean(x * x, axis=-1, keepdims=True) + EPS) * g


def _bucket_np(dist):
    n = np.maximum(dist, 0)
    nf = np.maximum(n, 1).astype(np.float32)
    large = 16 + (np.log(nf / np.float32(16)) / np.float32(math.log(8.0)) * np.float32(16)).astype(np.int32)
    large = np.minimum(large, N_BUCKETS - 1)
    return np.where(n < 16, n, large).astype(np.int32)


def _bucket_tiles():
    i = np.arange(LANE)[:, None]
    c = np.arange(LANE)[None, :]
    diag = np.where(i >= c, _bucket_np(i - c), -1)
    prev = _bucket_np(LANE + i - c)
    swa_prev = np.where(c >= i, _bucket_np(LANE + i - c), -1)
    dec = _bucket_np(LANE - c + 0 * i)
    return np.stack([diag, prev, diag, swa_prev, dec, dec]).astype(np.int32)


def _bias_tiles_body(rb_ref, d_ref, o_ref):
    h = pl.program_id(0)
    far = rb_ref[N_BUCKETS - 1, h]
    for t in range(6):
        d = d_ref[t]
        acc = jnp.zeros((LANE, LANE), F32)
        for b in range(N_BUCKETS):
            acc = jnp.where(d == b, rb_ref[b, h], acc)
        if t in _SHIFTED_TILES:
            acc = acc - far
        o_ref[t, 0] = jnp.where(d < 0, NEG, acc * LOG2E)


def _bias_tiles(rel_bias):
    d = jnp.asarray(_bucket_tiles())
    return pl.pallas_call(
        _bias_tiles_body,
        grid=(N_HEADS,),
        in_specs=[pl.BlockSpec(memory_space=pltpu.SMEM),
                  pl.BlockSpec((6, LANE, LANE), lambda h: (0, 0, 0))],
        out_specs=pl.BlockSpec((6, 1, LANE, LANE), lambda h: (0, h, 0, 0)),
        out_shape=jax.ShapeDtypeStruct((6, N_HEADS, LANE, LANE), F32),
        compiler_params=_cparams(("arbitrary",)),
        name="bias_tiles",
    )(rel_bias.astype(F32), d)


def _proj_body(*refs, n_w, norm, residual):
    it = iter(refs)
    x_ref = next(it)
    g_ref = next(it) if norm else None
    r_ref = next(it) if residual else None
    w_refs = [next(it) for _ in range(n_w)]
    o_refs = [next(it) for _ in range(n_w)]
    x = x_ref[...]
    if norm:
        x = _rms(x.astype(F32), g_ref[...])
    xb = x.astype(BF16)
    for k in range(n_w):
        acc = jnp.dot(xb, w_refs[k][...], preferred_element_type=F32)
        if residual and k == 0:
            acc = acc + r_ref[...]
        o_refs[k][...] = acc.astype(o_refs[k].dtype)


def _proj(x, ws, out_dtypes, g=None, res=None, tm=512, name="proj"):
    t, kdim = x.shape
    tm = min(tm, t)
    args = [x]
    specs = [pl.BlockSpec((tm, kdim), lambda i: (i, 0))]
    if g is not None:
        args.append(g.reshape(1, kdim).astype(F32))
        specs.append(pl.BlockSpec((1, kdim), lambda i: (0, 0)))
    if res is not None:
        args.append(res)
        specs.append(pl.BlockSpec((tm, res.shape[1]), lambda i: (i, 0)))
    for w in ws:
        args.append(w)
        specs.append(pl.BlockSpec(w.shape, lambda i: (0, 0)))
    outs = pl.pallas_call(
        functools.partial(_proj_body, n_w=len(ws), norm=g is not None, residual=res is not None),
        grid=(t // tm,),
        in_specs=specs,
        out_specs=[pl.BlockSpec((tm, w.shape[1]), lambda i: (i, 0)) for w in ws],
        out_shape=[jax.ShapeDtypeStruct((t, w.shape[1]), dt) for w, dt in zip(ws, out_dtypes)],
        compiler_params=_cparams(("arbitrary",), VMEM_LIMIT),
        name=name,
    )(*args)
    return outs


def _norm_body(x_ref, g_ref, o_ref):
    o_ref[...] = _rms(x_ref[...], g_ref[...])


def _final_norm(x, g, tm=512):
    t, d = x.shape
    tm = min(tm, t)
    return pl.pallas_call(
        _norm_body,
        grid=(t // tm,),
        in_specs=[pl.BlockSpec((tm, d), lambda i: (i, 0)), pl.BlockSpec((1, d), lambda i: (0, 0))],
        out_specs=pl.BlockSpec((tm, d), lambda i: (i, 0)),
        out_shape=jax.ShapeDtypeStruct((t, d), F32),
        compiler_params=_cparams(("arbitrary",)),
        name="final_norm",
    )(x, g.reshape(1, d))


def _headwise_body(x_ref, w_ref, o_ref):
    o_ref[...] = jnp.dot(x_ref[...].astype(BF16), w_ref[0], preferred_element_type=F32).astype(o_ref.dtype)


def _headwise(x, w, out_dtype, name):
    m = x.shape[0]
    n, kb, nb = w.shape
    return pl.pallas_call(
        _headwise_body,
        grid=(n,),
        in_specs=[pl.BlockSpec((m, kb), lambda i: (0, i)), pl.BlockSpec((1, kb, nb), lambda i: (i, 0, 0))],
        out_specs=pl.BlockSpec((m, nb), lambda i: (0, i)),
        out_shape=jax.ShapeDtypeStruct((m, n * nb), out_dtype),
        compiler_params=_cparams(("arbitrary",)),
        name=name,
    )(x, w)


def _mla_stage2_body(s1_ref, gq_ref, gkv_ref, cq_ref, sq_ref, ck_ref, wuq_ref, wuk_ref, wuv_ref,
                     q_ref, k_ref, v_ref, lat_ref, kpe_ref):
    s1 = s1_ref[...]
    cosq, sinq, cosk = cq_ref[...], sq_ref[...], ck_ref[...]
    cqn = _rms(s1[:, :MLA_Q_LORA], gq_ref[...]).astype(BF16)
    qa = jnp.dot(cqn, wuq_ref[...], preferred_element_type=F32)
    for h in range(N_HEADS):
        blk = qa[:, h * LANE:(h + 1) * LANE]
        q_ref[:, h * LANE:(h + 1) * LANE] = (blk * cosq + pltpu.roll(blk, LANE - MLA_ROPE, axis=1) * sinq).astype(BF16)
    lat = _rms(s1[:, MLA_Q_LORA:MLA_Q_LORA + MLA_KV_LORA], gkv_ref[...])
    lat_ref[...] = lat
    latb = lat.astype(BF16)
    pe = s1[:, MLA_Q_LORA + MLA_KV_LORA:]
    kpe = pltpu.roll(pe, 2 * MLA_ROPE, axis=1) * cosk + pltpu.roll(pe, MLA_ROPE, axis=1) * sinq
    kpe_ref[...] = kpe
    kn = jnp.dot(latb, wuk_ref[...], preferred_element_type=F32)
    vn = jnp.dot(latb, wuv_ref[...], preferred_element_type=F32)
    lane = lax.broadcasted_iota(jnp.int32, kpe.shape, 1)
    ones = jnp.where(lane == ONES_LANE, 1.0, 0.0)
    for h in range(N_HEADS):
        k_ref[:, h * LANE:(h + 1) * LANE] = (kn[:, h * LANE:(h + 1) * LANE] + kpe).astype(BF16)
        v_ref[:, h * LANE:(h + 1) * LANE] = (vn[:, h * LANE:(h + 1) * LANE] + ones).astype(BF16)


def _mla_stage2(s1, g_q, g_kv, cosq, sinq, cosk, wuq, wuk, wuv, tm=256):
    t = s1.shape[0]
    tm = min(tm, t)
    ntab = cosq.shape[0] // tm
    row = lambda i: (i, 0)
    tab = lambda i: (i % ntab, 0)
    full = lambda i: (0, 0)
    hw = N_HEADS * LANE
    return pl.pallas_call(
        _mla_stage2_body,
        grid=(t // tm,),
        in_specs=[pl.BlockSpec((tm, s1.shape[1]), row),
                  pl.BlockSpec((1, MLA_Q_LORA), full), pl.BlockSpec((1, MLA_KV_LORA), full),
                  pl.BlockSpec((tm, LANE), tab), pl.BlockSpec((tm, LANE), tab), pl.BlockSpec((tm, LANE), tab),
                  pl.BlockSpec(wuq.shape, full), pl.BlockSpec(wuk.shape, full), pl.BlockSpec(wuv.shape, full)],
        out_specs=[pl.BlockSpec((tm, hw), row), pl.BlockSpec((tm, hw), row), pl.BlockSpec((tm, hw), row),
                   pl.BlockSpec((tm, MLA_KV_LORA), row), pl.BlockSpec((tm, LANE), row)],
        out_shape=[jax.ShapeDtypeStruct((t, hw), BF16), jax.ShapeDtypeStruct((t, hw), BF16),
                   jax.ShapeDtypeStruct((t, hw), BF16),
                   jax.ShapeDtypeStruct((t, MLA_KV_LORA), F32), jax.ShapeDtypeStruct((t, LANE), F32)],
        compiler_params=_cparams(("arbitrary",), VMEM_LIMIT),
        name="mla_stage2",
    )(s1, g_q.reshape(1, -1), g_kv.reshape(1, -1), cosq, sinq, cosk, wuq, wuk, wuv)


QK_LOOKAHEAD = 4


def _flash_scores(qs, ks, s_wr):
    r = qs[0].shape[0]
    for c, q in enumerate(qs):
        s_wr[c * r:(c + 1) * r, :] = _nt_dot(q, ks[c])


def _flash_stage(qs, ks_next, vs, s_sc, m_sc, acc_sc, biases=None):
    n = len(qs)
    r = qs[0].shape[0]
    pending = {}

    def scores(c):
        if ks_next is not None and c < n:
            pending[c] = _nt_dot(qs[c], ks_next[c])

    m_all, acc_all = m_sc[...], acc_sc[...]
    new_m, new_acc = [], []
    for c in range(QK_LOOKAHEAD):
        scores(c)
    for c in range(n):
        s = s_sc[c * r:(c + 1) * r, :]
        if c in pending:
            s_sc[c * r:(c + 1) * r, :] = pending.pop(c)
        if biases is not None:
            s = s + biases[c]
        cols = [s[:, w * LANE:(w + 1) * LANE] for w in range(s.shape[1] // LANE)]
        mx = cols[0]
        for col in cols[1:]:
            mx = jnp.maximum(mx, col)
        m_prev = m_all[c * r:(c + 1) * r]
        m_new = jnp.maximum(m_prev, jnp.max(mx, axis=1, keepdims=True))
        alpha = jnp.exp2(m_prev - m_new)
        p = jnp.concatenate([jnp.exp2(col - m_new) for col in cols], axis=1).astype(BF16)
        new_acc.append(alpha * acc_all[c * r:(c + 1) * r] + jnp.dot(p, vs[c], preferred_element_type=F32))
        new_m.append(m_new)
        scores(c + QK_LOOKAHEAD)
    m_sc[...] = jnp.concatenate(new_m, axis=0)
    acc_sc[...] = jnp.concatenate(new_acc, axis=0)


def _init_flash(m_sc, acc_sc):
    m_sc[...] = jnp.full(m_sc.shape, NEG, F32)
    acc_sc[...] = jnp.zeros(acc_sc.shape, F32)


def _normalised(acc):
    return acc / acc[:, ONES_LANE:ONES_LANE + 1]


def _pair_store(o_ref, pr, left, right):
    lane = lax.broadcasted_iota(jnp.int32, left.shape, 1)
    o_ref[0, :, pr * LANE:(pr + 1) * LANE] = jnp.where(
        lane < HEAD_DIM, left, pltpu.roll(right, HEAD_DIM, axis=1)).astype(o_ref.dtype)


def _mla_prompt_body(q_ref, k_ref, v_ref, o_ref, m_sc, acc_sc, s_sc, *, tq, r):
    qi = pl.program_id(2)
    nrb = tq // r
    qs = [q_ref[0, rb * r:(rb + 1) * r, hh * LANE:(hh + 1) * LANE] for hh in range(2) for rb in range(nrb)]
    _init_flash(m_sc, acc_sc)

    def chunk(ref, start):
        return [ref[0, pl.ds(start, tq), hh * LANE:(hh + 1) * LANE] for hh in range(2) for _ in range(nrb)]

    _flash_scores(qs, chunk(k_ref, 0), s_sc)

    def far(j, carry):
        _flash_stage(qs, chunk(k_ref, pl.multiple_of((j + 1) * tq, tq)), chunk(v_ref, pl.multiple_of(j * tq, tq)),
                     s_sc, m_sc, acc_sc)
        return carry

    lax.fori_loop(0, qi, far, 0)
    row = lax.broadcasted_iota(jnp.int32, (r, tq), 0)
    col = lax.broadcasted_iota(jnp.int32, (r, tq), 1)
    causal = [jnp.where(col <= row + rb * r, 0.0, NEG) for rb in range(nrb)] * 2
    _flash_stage(qs, None, chunk(v_ref, pl.multiple_of(qi * tq, tq)), s_sc, m_sc, acc_sc, causal)
    for rb in range(nrb):
        left = _normalised(acc_sc[rb * r:(rb + 1) * r])
        right = _normalised(acc_sc[(nrb + rb) * r:(nrb + rb + 1) * r])
        lane = lax.broadcasted_iota(jnp.int32, left.shape, 1)
        o_ref[0, rb * r:(rb + 1) * r, :] = jnp.where(
            lane < HEAD_DIM, left, pltpu.roll(right, HEAD_DIM, axis=1)).astype(o_ref.dtype)


def _mla_prompt_attn(qs, ks, vs, b, s, tq=CHUNK, r=128):
    tq = min(tq, s)
    hw = N_HEADS * LANE
    qs, ks, vs = qs.reshape(b, s, hw), ks.reshape(b, s, hw), vs.reshape(b, s, hw)
    nch = 2 * (tq // r)
    return pl.pallas_call(
        functools.partial(_mla_prompt_body, tq=tq, r=r),
        grid=(b, N_HEADS // 2, s // tq),
        in_specs=[pl.BlockSpec((1, tq, 2 * LANE), lambda bi, hp, qi: (bi, qi, hp)),
                  pl.BlockSpec((1, s, 2 * LANE), lambda bi, hp, qi: (bi, 0, hp)),
                  pl.BlockSpec((1, s, 2 * LANE), lambda bi, hp, qi: (bi, 0, hp))],
        out_specs=pl.BlockSpec((1, tq, LANE), lambda bi, hp, qi: (bi, qi, hp)),
        out_shape=jax.ShapeDtypeStruct((b, s, N_HEADS * HEAD_DIM), BF16),
        scratch_shapes=[pltpu.VMEM((nch * r, LANE), F32)] * 2 + [pltpu.VMEM((nch * r, tq), F32)],
        compiler_params=_cparams(("arbitrary", "arbitrary", "arbitrary"), VMEM_LIMIT),
        name="mla_prompt_attn",
    )(qs, ks, vs)


def _lambda_value(lamv, lam_init):
    a = jnp.sum(lamv[0:1] * lamv[1:2], axis=1, keepdims=True)
    c = jnp.sum(lamv[2:3] * lamv[3:4], axis=1, keepdims=True)
    return jnp.exp(a) - jnp.exp(c) + lam_init


def _with_ones(v):
    lane = lax.broadcasted_iota(jnp.int32, v.shape, 1)
    return jnp.where(lane == ONES_LANE, 1.0, v.astype(F32)).astype(BF16)


def _gqa_prompt_body(*refs, kind, tq, w, lam_init):
    if kind == "diff":
        q_ref, k_ref, v_ref, bias_ref, lamv_ref, gh_ref, o_ref, q_sc, m_sc, acc_sc, s_sc, v_sc = refs
        nmap = 2
    else:
        q_ref, k_ref, v_ref, bias_ref, o_ref, q_sc, m_sc, acc_sc, s_sc, v_sc, k_sc, kmh_sc, kml_sc = refs
        nmap = 1
    qi = pl.program_id(2)
    r = LANE
    nrb = tq // r
    nch = nmap * GROUP * nrb
    s_len = k_ref.shape[1]
    bpc = w // r
    lane_t = lax.broadcasted_iota(jnp.int32, (r, LANE), 1)

    def q_block(hh, rb):
        return q_ref[0, rb * r:(rb + 1) * r, hh * LANE:(hh + 1) * LANE]

    @pl.when(qi == 0)
    def _():
        for n in range(s_len // MOBA_BLOCK):
            sl = slice(n * MOBA_BLOCK, (n + 1) * MOBA_BLOCK)
            v_sc[sl, :] = _with_ones(v_ref[0, sl, :])
            if kind == "moba":
                kb = k_ref[0, sl, :].astype(F32)
                lane_k = lax.broadcasted_iota(jnp.int32, kb.shape, 1)
                k_sc[sl, :] = jnp.where(lane_k == HEAD_DIM + n, 1.0, kb).astype(BF16)
                if n == 0:
                    kmh_sc[...] = jnp.zeros(kmh_sc.shape, BF16)
                    kml_sc[...] = jnp.zeros(kml_sc.shape, BF16)
                km = jnp.sum(kb, axis=0, keepdims=True) * (1.0 / MOBA_BLOCK)
                hi = km.astype(BF16)
                kmh_sc[HEAD_DIM + n:HEAD_DIM + n + 1, :] = hi
                kml_sc[HEAD_DIM + n:HEAD_DIM + n + 1, :] = (km - hi.astype(F32)).astype(BF16)

    if kind == "diff":
        for mp in range(2):
            keep = (lane_t >= mp * DIFF_QK) & (lane_t < (mp + 1) * DIFF_QK)
            for hh in range(GROUP):
                for rb in range(nrb):
                    r0 = ((mp * GROUP + hh) * nrb + rb) * r
                    q_sc[r0:r0 + r, :] = jnp.where(keep, q_block(hh, rb).astype(F32), 0.0).astype(BF16)
    else:
        nblk = s_len // MOBA_BLOCK
        rows = GROUP * tq
        q0 = jnp.concatenate([q_block(hh, rb) for hh in range(GROUP) for rb in range(nrb)], axis=0)
        gate = _nt_dot(q0, kmh_sc[...]) + _nt_dot(q0, kml_sc[...])
        lane_r = lax.broadcasted_iota(jnp.int32, (rows, LANE), 1)
        lane_f = lane_r.astype(F32)
        q_blk = (qi * tq) // MOBA_BLOCK
        elig = (lane_r >= HEAD_DIM) & (lane_r < HEAD_DIM + q_blk)
        gsel = jnp.where(elig, gate, NEG)
        pen = jnp.where((lane_r >= HEAD_DIM) & (lane_r < HEAD_DIM + nblk), NEG, 0.0)
        for _ in range(MOBA_TOPK):
            mx = jnp.max(gsel, axis=1, keepdims=True)
            idx = jnp.min(jnp.where(gsel == mx, lane_f, 1e9), axis=1, keepdims=True)
            pick = (lane_f == idx) & (mx > 0.5 * NEG)
            pen = jnp.where(pick, 0.0, pen)
            gsel = jnp.where(pick, NEG, gsel)
        pen = jnp.where(lane_r == HEAD_DIM + q_blk, 0.0, pen)
        q_sc[...] = (q0.astype(F32) + pen).astype(BF16)

    qs = [q_sc[c * r:(c + 1) * r, :] for c in range(nch)]
    _init_flash(m_sc, acc_sc)
    kk_ref = k_sc if kind == "moba" else k_ref.at[0]

    def chunk_bias(j):
        per_block = {}
        for hh in range(GROUP):
            diag = bias_ref[0, 0, hh * r:(hh + 1) * r, :]
            prev = bias_ref[1, 0, hh * r:(hh + 1) * r, :]
            for rb in range(nrb):
                rel0 = j * bpc - (qi * nrb + rb)
                per_block[hh, rb] = jnp.concatenate(
                    [jnp.where(rel0 + t == 0, diag, jnp.where(rel0 + t == -1, prev, jnp.where(rel0 + t > 0, NEG, 0.0)))
                     for t in range(bpc)], axis=1)
        return [per_block[hh, rb] for _ in range(nmap) for hh in range(GROUP) for rb in range(nrb)]

    def stage(j, has_next, biased):
        ks = [kk_ref[pl.ds(pl.multiple_of((j + 1) * w, w), w), :]] * nch if has_next else None
        v = v_sc[pl.ds(pl.multiple_of(j * w, w), w), :]
        _flash_stage(qs, ks, [v] * nch, s_sc, m_sc, acc_sc, chunk_bias(j) if biased else None)

    qb0 = qi * nrb
    jl = (qb0 + nrb - 1) // bpc
    prev_in_earlier = jnp.logical_and(jl >= 1, jl * bpc >= qb0)
    n_plain = jnp.where(prev_in_earlier, jl - 1, jl)

    _flash_scores(qs, [kk_ref[0:w, :]] * nch, s_sc)

    def plain(j, carry):
        stage(j, True, False)
        return carry

    lax.fori_loop(0, n_plain, plain, 0)

    @pl.when(prev_in_earlier)
    def _():
        stage(jl - 1, True, True)

    stage(jl, False, True)

    def chain_rows(ref, mp, hh, rb):
        c = (mp * GROUP + hh) * nrb + rb
        return ref[c * r:(c + 1) * r]

    for rb in range(nrb):
        heads = []
        for hh in range(GROUP):
            if kind == "diff":
                lam = _lambda_value(lamv_ref[...], lam_init)
                a = _normalised(chain_rows(acc_sc, 0, hh, rb)) - lam * _normalised(chain_rows(acc_sc, 1, hh, rb))
                a = jnp.where(lane_t < HEAD_DIM, a, 0.0)
                ms = jnp.sum(a * a, axis=1, keepdims=True) * (1.0 / HEAD_DIM)
                heads.append(a * lax.rsqrt(ms + EPS) * gh_ref[...] * (1.0 - lam_init))
            else:
                heads.append(_normalised(chain_rows(acc_sc, 0, hh, rb)))
        for pr in range(GROUP // 2):
            lane = lax.broadcasted_iota(jnp.int32, (r, LANE), 1)
            o_ref[0, rb * r:(rb + 1) * r, pr * LANE:(pr + 1) * LANE] = jnp.where(
                lane < HEAD_DIM, heads[2 * pr], pltpu.roll(heads[2 * pr + 1], HEAD_DIM, axis=1)).astype(o_ref.dtype)


def _gqa_prompt_attn(slots, bias, b, s, kind, lamv=None, gh=None, lam_init=0.0, tq=256, w=CHUNK):
    w = min(w, s)
    assert s % w == 0 and w % tq == 0 and tq % LANE == 0 and s % MOBA_BLOCK == 0 and s // MOBA_BLOCK <= 32
    assert MOBA_BLOCK % tq == 0
    nslot = slots.shape[1] // LANE
    slots = slots.reshape(b, s, nslot * LANE)
    nmap = 2 if kind == "diff" else 1
    rows = nmap * GROUP * tq
    args = [slots, slots, slots, bias]
    specs = [pl.BlockSpec((1, tq, GROUP * LANE), lambda bi, g, qi: (bi, qi, g)),
             pl.BlockSpec((1, s, LANE), lambda bi, g, qi: (bi, 0, N_HEADS + g)),
             pl.BlockSpec((1, s, LANE), lambda bi, g, qi: (bi, 0, N_HEADS + N_KV + g)),
             pl.BlockSpec((2, 1, GROUP * LANE, LANE), lambda bi, g, qi: (0, g, 0, 0))]
    scratch = [pltpu.VMEM((rows, LANE), BF16), pltpu.VMEM((rows, LANE), F32), pltpu.VMEM((rows, LANE), F32),
               pltpu.VMEM((rows, w), F32), pltpu.VMEM((s, LANE), BF16)]
    if kind == "diff":
        args += [lamv, gh]
        specs += [pl.BlockSpec((4, LANE), lambda bi, g, qi: (0, 0)), pl.BlockSpec((1, LANE), lambda bi, g, qi: (0, 0))]
    else:
        scratch += [pltpu.VMEM((s, LANE), BF16), pltpu.VMEM((LANE, LANE), BF16), pltpu.VMEM((LANE, LANE), BF16)]
    return pl.pallas_call(
        functools.partial(_gqa_prompt_body, kind=kind, tq=tq, w=w, lam_init=lam_init),
        grid=(b, N_KV, s // tq),
        in_specs=specs,
        out_specs=pl.BlockSpec((1, tq, GROUP * HEAD_DIM), lambda bi, g, qi: (bi, qi, g)),
        out_shape=jax.ShapeDtypeStruct((b, s, N_HEADS * HEAD_DIM), BF16),
        scratch_shapes=scratch,
        compiler_params=_cparams(("arbitrary", "arbitrary", "arbitrary"), VMEM_LIMIT),
        name=kind + "_prompt_attn",
    )(*args)


def _swa_prompt_body(sink_ref, q_ref, kp_ref, kc_ref, vp_ref, vc_ref, bias_ref, o_ref, *, tq):
    g = pl.program_id(1)
    qi = pl.program_id(2)
    q = jnp.concatenate([q_ref[0, :, hh * LANE:(hh + 1) * LANE] for hh in range(GROUP)], axis=0)
    sink = jnp.concatenate([jnp.full((tq, LANE), sink_ref[g * GROUP + hh] * LOG2E, F32) for hh in range(GROUP)], axis=0)
    s_prev = _nt_dot(q, kp_ref[0]) + bias_ref[1, 0]
    s_prev = jnp.where(qi == 0, NEG, s_prev)
    s_cur = _nt_dot(q, kc_ref[0]) + bias_ref[0, 0]
    m = jnp.maximum(jnp.max(jnp.maximum(s_prev, s_cur), axis=1, keepdims=True), sink)
    p_prev = jnp.exp2(s_prev - m)
    p_cur = jnp.exp2(s_cur - m)
    den = jnp.sum(p_prev + p_cur, axis=1, keepdims=True) + jnp.exp2(sink - m)
    o = (jnp.dot(p_prev.astype(BF16), vp_ref[0], preferred_element_type=F32)
         + jnp.dot(p_cur.astype(BF16), vc_ref[0], preferred_element_type=F32)) / den
    for pr in range(GROUP // 2):
        _pair_store(o_ref, pr, o[2 * pr * tq:(2 * pr + 1) * tq], o[(2 * pr + 1) * tq:(2 * pr + 2) * tq])


def _swa_prompt_attn(slots, bias, sinks, b, s, tq=128):
    nslot = slots.shape[1] // LANE
    slots = slots.reshape(b, s, nslot * LANE)
    prev = lambda off: (lambda bi, g, qi: (bi, jnp.maximum(qi - 1, 0), off + g))
    cur = lambda off: (lambda bi, g, qi: (bi, qi, off + g))
    return pl.pallas_call(
        functools.partial(_swa_prompt_body, tq=tq),
        grid=(b, N_KV, s // tq),
        in_specs=[pl.BlockSpec(memory_space=pltpu.SMEM),
                  pl.BlockSpec((1, tq, GROUP * LANE), lambda bi, g, qi: (bi, qi, g)),
                  pl.BlockSpec((1, tq, LANE), prev(N_HEADS)), pl.BlockSpec((1, tq, LANE), cur(N_HEADS)),
                  pl.BlockSpec((1, tq, LANE), prev(N_HEADS + N_KV)), pl.BlockSpec((1, tq, LANE), cur(N_HEADS + N_KV)),
                  pl.BlockSpec((2, 1, GROUP * tq, LANE), lambda bi, g, qi: (0, g, 0, 0))],
        out_specs=pl.BlockSpec((1, tq, GROUP * HEAD_DIM), lambda bi, g, qi: (bi, qi, g)),
        out_shape=jax.ShapeDtypeStruct((b, s, N_HEADS * HEAD_DIM), BF16),
        compiler_params=_cparams(("arbitrary", "arbitrary", "arbitrary")),
        name="swa_prompt_attn",
    )(sinks.astype(F32), slots, slots, slots, slots, slots, bias)


def _silu(x):
    return x / (1.0 + jnp.exp(-x))


def _ffn_prompt_body(x_ref, g_ref, wg_ref, wu_ref, cwg_ref, cwu_ref, cbg_ref, cbu_ref, wd_ref,
                     o_ref, sg_ref, su_ref, xn_sc, acc_sc, carry_sc, *, tiles_per_seq):
    i = pl.program_id(0)
    j = pl.program_id(1)
    tm = x_ref.shape[0]

    @pl.when(j == 0)
    def _():
        xn_sc[...] = _rms(x_ref[...], g_ref[...]).astype(BF16)
        acc_sc[...] = jnp.zeros(acc_sc.shape, F32)

    xb = xn_sc[...]
    first = (i % tiles_per_seq) == 0
    row = lax.broadcasted_iota(jnp.int32, (tm, wg_ref.shape[1]), 0)

    @pl.when(first)
    def _():
        carry_sc[j] = jnp.zeros(carry_sc.shape[1:], F32)

    def conv(h, cw_ref, cb_ref, idx):
        prev = carry_sc[j, idx]
        p6, p7 = prev[6:7], prev[7:8]
        h1 = jnp.where(row == 0, p7, pltpu.roll(h, 1, axis=0))
        h2 = jnp.where(row == 0, p6, jnp.where(row == 1, p7, pltpu.roll(h, 2, axis=0)))
        carry_sc[j, idx] = h[tm - 8:]
        return cb_ref[...] + cw_ref[0:1] * h2 + cw_ref[1:2] * h1 + cw_ref[2:3] * h

    hg = jnp.dot(xb, wg_ref[...], preferred_element_type=F32)
    hu = jnp.dot(xb, wu_ref[...], preferred_element_type=F32)
    sg_ref[0] = hg[tm - 8:]
    su_ref[0] = hu[tm - 8:]
    act = _silu(conv(hg, cwg_ref, cbg_ref, 0)) * conv(hu, cwu_ref, cbu_ref, 1)
    acc_sc[...] += jnp.dot(act.astype(BF16), wd_ref[...], preferred_element_type=F32)

    @pl.when(j == pl.num_programs(1) - 1)
    def _():
        o_ref[...] = x_ref[...] + acc_sc[...]


def _ffn_prompt(x, g, w_up, conv_w, conv_b, w_down, b, s, tm=1024, fc=256):
    t, d = x.shape
    dff = w_down.shape[0]
    tm = min(tm, s)
    nf = dff // fc
    row = lambda i, j: (i, 0)
    gcol = lambda i, j: (0, j)
    ucol = lambda i, j: (0, nf + j)
    tps = s // tm
    out, sg, su = pl.pallas_call(
        functools.partial(_ffn_prompt_body, tiles_per_seq=tps),
        grid=(t // tm, nf),
        in_specs=[pl.BlockSpec((tm, d), row), pl.BlockSpec((1, d), lambda i, j: (0, 0)),
                  pl.BlockSpec((d, fc), gcol), pl.BlockSpec((d, fc), ucol),
                  pl.BlockSpec((3, fc), gcol), pl.BlockSpec((3, fc), ucol),
                  pl.BlockSpec((1, fc), gcol), pl.BlockSpec((1, fc), ucol),
                  pl.BlockSpec((fc, d), lambda i, j: (j, 0))],
        out_specs=[pl.BlockSpec((tm, d), row),
                   pl.BlockSpec((1, 8, fc), lambda i, j: (i, 0, j)),
                   pl.BlockSpec((1, 8, fc), lambda i, j: (i, 0, j))],
        out_shape=[jax.ShapeDtypeStruct((t, d), F32), jax.ShapeDtypeStruct((t // tm, 8, dff), F32),
                   jax.ShapeDtypeStruct((t // tm, 8, dff), F32)],
        scratch_shapes=[pltpu.VMEM((tm, d), BF16), pltpu.VMEM((tm, d), F32), pltpu.VMEM((nf, 2, 8, fc), F32)],
        compiler_params=_cparams(("arbitrary", "arbitrary"), VMEM_LIMIT),
        name="ffn_prompt",
    )(x, g.reshape(1, d), w_up, w_up, conv_w, conv_w, conv_b.reshape(1, -1), conv_b.reshape(1, -1), w_down)
    last = slice(tps - 1, None, tps)
    state = jnp.concatenate([sg[last, 6:8], su[last, 6:8]], axis=-1)
    return out, state


def _ffn_sample_body(x_ref, g_ref, wg_ref, wu_ref, cwg_ref, cwu_ref, cbg_ref, cbu_ref, wd_ref,
                     s0g_ref, s1g_ref, s0u_ref, s1u_ref, o_ref, hg_ref, hu_ref, xn_sc, acc_sc):
    j = pl.program_id(0)

    @pl.when(j == 0)
    def _():
        xn_sc[...] = _rms(x_ref[...], g_ref[...]).astype(BF16)
        acc_sc[...] = jnp.zeros(acc_sc.shape, F32)

    xb = xn_sc[...]
    hg = jnp.dot(xb, wg_ref[...], preferred_element_type=F32)
    hu = jnp.dot(xb, wu_ref[...], preferred_element_type=F32)
    hg_ref[...] = hg
    hu_ref[...] = hu
    cg = cbg_ref[...] + cwg_ref[0:1] * s0g_ref[...] + cwg_ref[1:2] * s1g_ref[...] + cwg_ref[2:3] * hg
    cu = cbu_ref[...] + cwu_ref[0:1] * s0u_ref[...] + cwu_ref[1:2] * s1u_ref[...] + cwu_ref[2:3] * hu
    acc_sc[...] += jnp.dot((_silu(cg) * cu).astype(BF16), wd_ref[...], preferred_element_type=F32)

    @pl.when(j == pl.num_programs(0) - 1)
    def _():
        o_ref[...] = x_ref[...] + acc_sc[...]


def _ffn_sample(x, g, w_up, conv_w, conv_b, w_down, state, fc=256):
    t, d = x.shape
    dff = w_down.shape[0]
    nf = dff // fc
    s0, s1 = state[:, 0], state[:, 1]
    full = lambda j: (0, 0)
    gcol = lambda j: (0, j)
    ucol = lambda j: (0, nf + j)
    out, hg, hu = pl.pallas_call(
        _ffn_sample_body,
        grid=(nf,),
        in_specs=[pl.BlockSpec((t, d), full), pl.BlockSpec((1, d), full),
                  pl.BlockSpec((d, fc), gcol), pl.BlockSpec((d, fc), ucol),
                  pl.BlockSpec((3, fc), gcol), pl.BlockSpec((3, fc), ucol),
                  pl.BlockSpec((1, fc), gcol), pl.BlockSpec((1, fc), ucol),
                  pl.BlockSpec((fc, d), lambda j: (j, 0)),
                  pl.BlockSpec((t, fc), gcol), pl.BlockSpec((t, fc), gcol),
                  pl.BlockSpec((t, fc), ucol), pl.BlockSpec((t, fc), ucol)],
        out_specs=[pl.BlockSpec((t, d), full), pl.BlockSpec((t, fc), gcol), pl.BlockSpec((t, fc), gcol)],
        out_shape=[jax.ShapeDtypeStruct((t, d), F32), jax.ShapeDtypeStruct((t, dff), F32),
                   jax.ShapeDtypeStruct((t, dff), F32)],
        scratch_shapes=[pltpu.VMEM((t, d), BF16), pltpu.VMEM((t, d), F32)],
        compiler_params=_cparams(("arbitrary",), VMEM_LIMIT),
        name="ffn_sample",
    )(x, g.reshape(1, d), w_up, w_up, conv_w, conv_w, conv_b.reshape(1, -1), conv_b.reshape(1, -1), w_down,
      s0, s1, s0, s1)
    h = jnp.concatenate([hg, hu], axis=-1)
    return out, jnp.stack([s1, h], axis=1)


def _page_copies(pt_ref, seq, slot, n_pages, page_off, pairs):
    def make(p, cache, buf, sem):
        pg = pt_ref[seq, p] + page_off
        return pltpu.make_async_copy(cache.at[pg], buf.at[slot, p], sem.at[slot])

    def start(p, c):
        for cache, buf, sem in pairs:
            make(p, cache, buf, sem).start()
        return c

    def wait(p, c):
        for cache, buf, sem in pairs:
            make(p, cache, buf, sem).wait()
        return c

    return (lambda: lax.fori_loop(0, n_pages, start, 0)), (lambda: lax.fori_loop(0, n_pages, wait, 0))


def _prefetch_pages(pt_ref, n_pages, page_off, pairs):
    b = pl.program_id(0)
    nb = pl.num_programs(0)

    @pl.when(b == 0)
    def _():
        _page_copies(pt_ref, 0, 0, n_pages, page_off, pairs)[0]()

    @pl.when(b + 1 < nb)
    def _():
        _page_copies(pt_ref, b + 1, (b + 1) % 2, n_pages, page_off, pairs)[0]()

    slot = b % 2
    _page_copies(pt_ref, b, slot, n_pages, page_off, pairs)[1]()
    return slot


def _softmax_pages(s_sc, p_sc, s_self, n_pages):
    s = s_sc[...]
    m = jnp.maximum(jnp.max(jnp.max(s, axis=0), axis=1, keepdims=True), s_self)
    p = jnp.exp2(s - m[None])
    p_self = jnp.exp2(s_self - m)
    p_sc[...] = p.astype(BF16)
    return p_self, jnp.sum(jnp.sum(p, axis=0), axis=1, keepdims=True) + p_self


def _gqa_decode_body(pt_ref, q_ref, knew_ref, vnew_ref, bdec_ref, bself_ref, *rest,
                     kind, n_pages, page_off, lam_init):
    if kind == "diff":
        lamv_ref, gh_ref, kc_hbm, vc_hbm, o_ref, kbuf, vbuf, s_sc, p_sc, ksem, vsem = rest
    else:
        kc_hbm, vc_hbm, o_ref, kbuf, vbuf, s_sc, p_sc, ksem, vsem = rest
    slot = _prefetch_pages(pt_ref, n_pages, page_off, [(kc_hbm, kbuf, ksem), (vc_hbm, vbuf, vsem)])
    q = q_ref[0]
    rows, kw = q.shape

    if kind == "moba":
        ppb = MOBA_BLOCK // PAGE
        nblk = n_pages // ppb
        lane_k = lax.broadcasted_iota(jnp.int32, (kw, LANE), 1)
        kmt = jnp.zeros((kw, LANE), F32)
        for n in range(nblk):
            blk = kbuf[slot, ppb * n]
            for e in range(1, ppb):
                blk = blk + kbuf[slot, ppb * n + e]
            kmt = jnp.where(lane_k == n, jnp.sum(blk, axis=1, keepdims=True) * (1.0 / MOBA_BLOCK), kmt)
        hi = kmt.astype(BF16)
        lo = (kmt - hi.astype(F32)).astype(BF16)
        gate = jnp.dot(q, hi, preferred_element_type=F32) + jnp.dot(q, lo, preferred_element_type=F32)
        lane_r = lax.broadcasted_iota(jnp.int32, (rows, LANE), 1)
        lane_f = lane_r.astype(F32)
        gsel = jnp.where(lane_r < nblk, gate, NEG)
        pen = jnp.where(lane_r < nblk, NEG, 0.0)
        for _ in range(MOBA_TOPK):
            mx = jnp.max(gsel, axis=1, keepdims=True)
            idx = jnp.min(jnp.where(gsel == mx, lane_f, 1e9), axis=1, keepdims=True)
            pick = (lane_f == idx) & (mx > 0.5 * NEG)
            pen = jnp.where(pick, 0.0, pen)
            gsel = jnp.where(pick, NEG, gsel)

    def scores(p, c):
        s = jnp.dot(q, kbuf[slot, p].astype(BF16), preferred_element_type=F32)
        if kind == "moba":
            s = s + jnp.sum(jnp.where(lane_r == p // ppb, pen, 0.0), axis=1, keepdims=True)
        s_sc[p] = s
        return c

    lax.fori_loop(0, n_pages, scores, 0, unroll=PAGE_UNROLL)
    s_sc[n_pages - 1] = s_sc[n_pages - 1] + bdec_ref[...]
    s_self = jnp.sum(q.astype(F32) * knew_ref[0], axis=1, keepdims=True) + bself_ref[:, 0:1]
    p_self, den = _softmax_pages(s_sc, p_sc, s_self, n_pages)

    def values(p, o):
        return o + _nt_dot(p_sc[p], vbuf[slot, p].astype(BF16))

    on = lax.fori_loop(0, n_pages, values, p_self * vnew_ref[0], unroll=PAGE_UNROLL) / den
    if kind == "diff":
        lam = _lambda_value(lamv_ref[...], lam_init)
        a = on[:N_HEADS] - lam * on[N_HEADS:]
        r16 = lax.broadcasted_iota(jnp.int32, (N_HEADS, kw), 0)
        l16 = lax.broadcasted_iota(jnp.int32, (N_HEADS, kw), 1)
        own = jnp.right_shift(l16, 6) == jnp.right_shift(r16, 2)
        ms = jnp.sum(jnp.where(own, a * a, 0.0), axis=1, keepdims=True) * (1.0 / HEAD_DIM)
        on = a * lax.rsqrt(ms + EPS) * gh_ref[...] * (1.0 - lam_init)
    o_ref[0] = on


def _pos_minor_pages(cache):
    nl, n_pool, page, nkv, hd = cache.shape
    return jnp.transpose(cache, (0, 1, 3, 4, 2)).reshape(nl * n_pool, nkv * hd, page)


def _gqa_decode_attn(page_table, qdec, k_new, v_new, bdec, bself, k_cache, v_cache, layer, kind,
                     lamv=None, gh=None, lam_init=0.0):
    db, n_pages = page_table.shape
    kw = N_KV * HEAD_DIM
    rows = qdec.shape[1] // kw
    n_pool = k_cache.shape[1]
    kc, vc = _pos_minor_pages(k_cache), _pos_minor_pages(v_cache)
    qdec = qdec.reshape(db, rows, kw)
    args = [qdec, k_new.reshape(db, 1, kw), v_new.reshape(db, 1, kw), bdec, bself]
    specs = [pl.BlockSpec((1, rows, kw), lambda b, pt: (b, 0, 0)),
             pl.BlockSpec((1, 1, kw), lambda b, pt: (b, 0, 0)),
             pl.BlockSpec((1, 1, kw), lambda b, pt: (b, 0, 0)),
             pl.BlockSpec((rows, LANE), lambda b, pt: (0, 0)),
             pl.BlockSpec((rows, LANE), lambda b, pt: (0, 0))]
    scratch = [pltpu.VMEM((2, n_pages, kw, PAGE), F32), pltpu.VMEM((2, n_pages, kw, PAGE), F32),
               pltpu.VMEM((n_pages, rows, PAGE), F32), pltpu.VMEM((n_pages, rows, PAGE), BF16)]
    if kind == "diff":
        args += [lamv, gh]
        specs += [pl.BlockSpec((4, LANE), lambda b, pt: (0, 0)), pl.BlockSpec((1, kw), lambda b, pt: (0, 0))]
    else:
        assert n_pages % (MOBA_BLOCK // PAGE) == 0 and n_pages // (MOBA_BLOCK // PAGE) <= LANE
    scratch += [pltpu.SemaphoreType.DMA((2,)), pltpu.SemaphoreType.DMA((2,))]
    args += [kc, vc]
    specs += [pl.BlockSpec(memory_space=pl.ANY), pl.BlockSpec(memory_space=pl.ANY)]
    return pl.pallas_call(
        functools.partial(_gqa_decode_body, kind=kind, n_pages=n_pages, page_off=layer * n_pool, lam_init=lam_init),
        grid_spec=pltpu.PrefetchScalarGridSpec(
            num_scalar_prefetch=1, grid=(db,), in_specs=specs,
            out_specs=pl.BlockSpec((1, N_HEADS, kw), lambda b, pt: (b, 0, 0)),
            scratch_shapes=scratch),
        out_shape=jax.ShapeDtypeStruct((db, N_HEADS, kw), F32),
        compiler_params=_cparams(("arbitrary",), VMEM_LIMIT),
        name=kind + "_decode_attn",
    )(page_table, *args)


def _mla_decode_body(pt_ref, ql_ref, qp_ref, lnew_ref, pnew_ref, lat_hbm, kpe_hbm, o_ref,
                     lbuf, pbuf, s_sc, p_sc, lsem, psem, *, n_pages, page_off):
    slot = _prefetch_pages(pt_ref, n_pages, page_off, [(lat_hbm, lbuf, lsem), (kpe_hbm, pbuf, psem)])
    ql = ql_ref[0]
    qp = qp_ref[0]

    def scores(p, c):
        s_sc[p] = (_nt_dot(ql, lbuf[slot, p].astype(BF16))
                   + jnp.dot(qp, pbuf[slot, p].astype(BF16), preferred_element_type=F32))
        return c

    lax.fori_loop(0, n_pages, scores, 0, unroll=PAGE_UNROLL)
    s_self = (jnp.sum(ql.astype(F32) * lnew_ref[0], axis=1, keepdims=True)
              + jnp.sum(qp.astype(F32) * pnew_ref[0], axis=1, keepdims=True))
    p_self, den = _softmax_pages(s_sc, p_sc, s_self, n_pages)

    def values(p, o):
        return o + jnp.dot(p_sc[p], lbuf[slot, p].astype(BF16), preferred_element_type=F32)

    o_ref[0] = lax.fori_loop(0, n_pages, values, p_self * lnew_ref[0], unroll=PAGE_UNROLL) / den


def _mla_decode_attn(page_table, q_lat, q_pe, lat_new, kpe_new, lat_cache, kpe_cache, layer):
    db, n_pages = page_table.shape
    n_pool = lat_cache.shape[1]
    lc = lat_cache.reshape(lat_cache.shape[0] * n_pool, PAGE, MLA_KV_LORA)
    pc = jnp.transpose(kpe_cache, (0, 1, 3, 2)).reshape(kpe_cache.shape[0] * n_pool, MLA_ROPE, PAGE)
    specs = [pl.BlockSpec((1, N_HEADS, MLA_KV_LORA), lambda b, pt: (b, 0, 0)),
             pl.BlockSpec((1, N_HEADS, MLA_ROPE), lambda b, pt: (b, 0, 0)),
             pl.BlockSpec((1, 1, MLA_KV_LORA), lambda b, pt: (b, 0, 0)),
             pl.BlockSpec((1, 1, MLA_ROPE), lambda b, pt: (b, 0, 0)),
             pl.BlockSpec(memory_space=pl.ANY), pl.BlockSpec(memory_space=pl.ANY)]
    scratch = [pltpu.VMEM((2, n_pages, PAGE, MLA_KV_LORA), F32), pltpu.VMEM((2, n_pages, MLA_ROPE, PAGE), F32),
               pltpu.VMEM((n_pages, N_HEADS, PAGE), F32), pltpu.VMEM((n_pages, N_HEADS, PAGE), BF16),
               pltpu.SemaphoreType.DMA((2,)), pltpu.SemaphoreType.DMA((2,))]
    return pl.pallas_call(
        functools.partial(_mla_decode_body, n_pages=n_pages, page_off=layer * n_pool),
        grid_spec=pltpu.PrefetchScalarGridSpec(
            num_scalar_prefetch=1, grid=(db,), in_specs=specs,
            out_specs=pl.BlockSpec((1, N_HEADS, MLA_KV_LORA), lambda b, pt: (b, 0, 0)),
            scratch_shapes=scratch),
        out_shape=jax.ShapeDtypeStruct((db, N_HEADS, MLA_KV_LORA), F32),
        compiler_params=_cparams(("arbitrary",), VMEM_LIMIT),
        name="mla_decode_attn",
    )(page_table, q_lat, q_pe, lat_new.reshape(db, 1, -1), kpe_new.reshape(db, 1, -1), lc, pc)


def _swa_decode_body(q_ref, k_ref, v_ref, knew_ref, vnew_ref, bdec_ref, bself_ref, sink_ref, o_ref, *, nseq):
    sink = sink_ref[:, 0:1] * LOG2E
    for i in range(nseq):
        q = q_ref[i]
        s = jnp.dot(q, k_ref[i].astype(BF16), preferred_element_type=F32) + bdec_ref[...]
        s_self = jnp.sum(q.astype(F32) * knew_ref[i], axis=1, keepdims=True) + bself_ref[:, 0:1]
        m = jnp.maximum(jnp.maximum(jnp.max(s, axis=1, keepdims=True), s_self), sink)
        p = jnp.exp2(s - m)
        p_self = jnp.exp2(s_self - m)
        den = jnp.sum(p, axis=1, keepdims=True) + p_self + jnp.exp2(sink - m)
        o = _nt_dot(p.astype(BF16), v_ref[i].astype(BF16)) + p_self * vnew_ref[i]
        o_ref[i] = o / den


def _swa_decode_attn(qdec, k_buf, v_buf, k_new, v_new, bdec, bself, sink_rows, nseq=8):
    db, w = k_buf.shape[:2]
    assert w == LANE
    kw = N_KV * HEAD_DIM
    nseq = min(nseq, db)
    qdec = qdec.reshape(db, N_HEADS, kw)
    kt = jnp.transpose(k_buf, (0, 2, 3, 1)).reshape(db, kw, w)
    vt = jnp.transpose(v_buf, (0, 2, 3, 1)).reshape(db, kw, w)
    seq = lambda b: (b, 0, 0)
    full = lambda b: (0, 0)
    return pl.pallas_call(
        functools.partial(_swa_decode_body, nseq=nseq),
        grid=(db // nseq,),
        in_specs=[pl.BlockSpec((nseq, N_HEADS, kw), seq),
                  pl.BlockSpec((nseq, kw, w), seq), pl.BlockSpec((nseq, kw, w), seq),
                  pl.BlockSpec((nseq, 1, kw), seq), pl.BlockSpec((nseq, 1, kw), seq),
                  pl.BlockSpec((N_HEADS, LANE), full), pl.BlockSpec((N_HEADS, LANE), full),
                  pl.BlockSpec((N_HEADS, LANE), full)],
        out_specs=pl.BlockSpec((nseq, N_HEADS, kw), seq),
        out_shape=jax.ShapeDtypeStruct((db, N_HEADS, kw), F32),
        compiler_params=_cparams(("arbitrary",)),
        name="swa_decode_attn",
    )(qdec, kt, vt, k_new.reshape(db, 1, kw), v_new.reshape(db, 1, kw), bdec, bself, sink_rows)


def _rot_cols(w):
    half = w.shape[-1] // 2
    return jnp.concatenate([-w[..., half:], w[..., :half]], axis=-1)


def _pad_lanes(w, width=LANE):
    return jnp.pad(w, [(0, 0)] * (w.ndim - 1) + [(0, width - w.shape[-1])])


def _gqa_weights(w_qkv, scale):
    d = w_qkv.shape[0]
    nq = N_HEADS * HEAD_DIM
    nk = N_KV * HEAD_DIM
    wq = w_qkv[:, :nq].reshape(d, N_HEADS, HEAD_DIM) * (scale * LOG2E)
    wk = w_qkv[:, nq:nq + nk].reshape(d, N_KV, HEAD_DIM)
    wv = w_qkv[:, nq + nk:].reshape(d, N_KV, HEAD_DIM)
    slots = jnp.concatenate([_pad_lanes(wq), _pad_lanes(wk), _pad_lanes(wv)], axis=1)
    return slots.reshape(d, -1).astype(BF16), w_qkv[:, nq:].astype(BF16), wq


def _decode_q_weights(wq, nmap):
    d = wq.shape[0]
    kw = N_KV * HEAD_DIM
    out = jnp.zeros((d, nmap, N_HEADS, kw), wq.dtype)
    for h in range(N_HEADS):
        base = (h // GROUP) * HEAD_DIM
        if nmap == 1:
            out = out.at[:, 0, h, base:base + HEAD_DIM].set(wq[:, h])
        else:
            for mp in range(2):
                out = out.at[:, mp, h, base + mp * DIFF_QK:base + (mp + 1) * DIFF_QK].set(
                    wq[:, h, mp * DIFF_QK:(mp + 1) * DIFF_QK])
    return out.reshape(d, -1).astype(BF16)


def _decode_o_weights(w_o):
    d = w_o.shape[1]
    kw = N_KV * HEAD_DIM
    out = jnp.zeros((N_HEADS, kw, d), w_o.dtype)
    for h in range(N_HEADS):
        base = (h // GROUP) * HEAD_DIM
        out = out.at[h, base:base + HEAD_DIM].set(w_o[h * HEAD_DIM:(h + 1) * HEAD_DIM])
    return out.reshape(N_HEADS * kw, d).astype(BF16)


def _rope_tables(pos):
    half = MLA_ROPE // 2
    inv = ROPE_THETA ** (-jnp.arange(half, dtype=F32) / half)
    ang = pos.astype(F32)[:, None] * inv[None, :]
    cos = jnp.concatenate([jnp.cos(ang)] * 2, axis=1)
    sin = jnp.concatenate([jnp.sin(ang)] * 2, axis=1)
    n = pos.shape[0]
    z64, z32, o64 = jnp.zeros((n, 64), F32), jnp.zeros((n, 32), F32), jnp.ones((n, 64), F32)
    cosq = jnp.concatenate([o64, cos, z32], axis=1)
    sinq = jnp.concatenate([z64, sin, z32], axis=1)
    cosk = jnp.concatenate([z64, cos, z32], axis=1)
    return cosq, sinq, cosk


def kernel(x_prompt, x_sample, cache_mla_latent, cache_mla_krope, cache_diff_k, cache_diff_v, cache_moba_k, cache_moba_v, state_swa_k, state_swa_v, state_ffn_conv, page_table, rel_bias, norm_mix_g, norm_ffn_g, norm_final_g, mla_w_dq, mla_g_q, mla_w_uq, mla_w_dkv, mla_g_kv, mla_w_uk, mla_w_uv, mla_w_o, diff_w_qkv, diff_lambda, diff_g_head, diff_w_o, moba_w_qkv, moba_w_o, swa_w_qkv, swa_sinks, swa_w_o, ffn_w_up, ffn_conv_w, ffn_conv_b, ffn_w_down):
    b, s, d = x_prompt.shape
    db = x_sample.shape[0]
    assert x_sample.shape[1] == 1
    depth = norm_mix_g.shape[0]
    past_len = page_table.shape[1] * PAGE
    kw = N_KV * HEAD_DIM
    np_pages = s // PAGE

    tiles = _bias_tiles(rel_bias)
    bias_causal = tiles[T_DIAG:T_PREV + 1].reshape(2, N_KV, GROUP * LANE, LANE)
    bias_swa = tiles[T_SWA_DIAG:T_SWA_PREV + 1].reshape(2, N_KV, GROUP * LANE, LANE)
    bdec = tiles[T_DEC, :, 0, :]
    bself = jnp.broadcast_to(tiles[T_DIAG, :, 0, 0:1], (N_HEADS, LANE))
    bdec_swa = tiles[T_SWA_DEC, :, 0, :]
    bself_swa = jnp.broadcast_to(tiles[T_SWA_DIAG, :, 0, 0:1], (N_HEADS, LANE))

    cos_p = _rope_tables(jnp.arange(s, dtype=jnp.int32))
    cos_s = tuple(jnp.broadcast_to(t, (db, LANE)) for t in _rope_tables(jnp.full((1,), past_len, jnp.int32)))

    hp = x_prompt.reshape(b * s, d)
    hs = x_sample.reshape(db, d)
    outs = {k: [] for k in ("lat_p", "lat_s", "kpe_p", "kpe_s", "dk_p", "dk_s", "dv_p", "dv_s", "mk_p", "mk_s",
                            "mv_p", "mv_s", "sk_p", "sk_s", "sv_p", "sv_s", "conv_p", "conv_s")}

    for i in range(depth):
        kind, l = i % 4, i // 4
        g_mix = norm_mix_g[i]
        if kind == 0:
            w_pe = mla_w_dkv[l][:, MLA_KV_LORA:]
            w1 = jnp.concatenate([mla_w_dq[l], mla_w_dkv[l][:, :MLA_KV_LORA], w_pe, _rot_cols(w_pe),
                                  jnp.zeros((d, LANE - 2 * MLA_ROPE), F32)], axis=1).astype(BF16)
            wuq = mla_w_uq[l] * (MLA_SCALE * LOG2E)
            wuq = jnp.concatenate([wuq, _rot_cols(wuq[..., MLA_NOPE:])], axis=-1).reshape(MLA_Q_LORA, -1).astype(BF16)
            wuk = _pad_lanes(mla_w_uk[l]).reshape(MLA_KV_LORA, -1).astype(BF16)
            wuv = _pad_lanes(mla_w_uv[l]).reshape(MLA_KV_LORA, -1).astype(BF16)
            w_o = mla_w_o[l].astype(BF16)
            (s1,) = _proj(hp, [w1], [F32], g=g_mix, name="mla_down_p")
            qs, ks, vs, lat, kpe = _mla_stage2(s1, mla_g_q[l], mla_g_kv[l], *cos_p, wuq, wuk, wuv)
            o = _mla_prompt_attn(qs, ks, vs, b, s)
            (hp,) = _proj(o.reshape(b * s, -1), [w_o], [F32], res=hp, name="mla_out_p")
            outs["lat_p"].append(lat.reshape(b, np_pages, PAGE, MLA_KV_LORA))
            outs["kpe_p"].append(kpe[:, HEAD_DIM:HEAD_DIM + MLA_ROPE].reshape(b, np_pages, PAGE, MLA_ROPE))
            (s1,) = _proj(hs, [w1], [F32], g=g_mix, name="mla_down_s")
            qs, _, _, lat, kpe = _mla_stage2(s1, mla_g_q[l], mla_g_kv[l], *cos_s, wuq, wuk, wuv)
            kpe = kpe[:, HEAD_DIM:HEAD_DIM + MLA_ROPE]
            wuk_t = jnp.pad(jnp.transpose(mla_w_uk[l], (1, 2, 0)),
                            ((0, 0), (0, LANE - MLA_NOPE), (0, 0))).astype(BF16)
            q_lat = _headwise(qs, wuk_t, BF16, "mla_q_absorb").reshape(db, N_HEADS, MLA_KV_LORA)
            q_pe = qs.reshape(db, N_HEADS, LANE)[:, :, HEAD_DIM:HEAD_DIM + MLA_ROPE]
            o_lat = _mla_decode_attn(page_table, q_lat, q_pe, lat, kpe, cache_mla_latent, cache_mla_krope, l)
            wuv_h = mla_w_uv[l].transpose(1, 0, 2)
            wuv_pair = jnp.concatenate(
                [jnp.concatenate([wuv_h[0::2], jnp.zeros_like(wuv_h[0::2])], axis=-1),
                 jnp.concatenate([jnp.zeros_like(wuv_h[1::2]), wuv_h[1::2]], axis=-1)], axis=1).astype(BF16)
            o = _headwise(o_lat.reshape(db, -1), wuv_pair, BF16, "mla_v_absorb")
            (hs,) = _proj(o, [w_o], [F32], res=hs, name="mla_out_s")
            outs["lat_s"].append(lat.reshape(db, 1, MLA_KV_LORA))
            outs["kpe_s"].append(kpe.reshape(db, 1, MLA_ROPE))
        else:
            w_qkv, w_o, scale, nmap = {
                1: (diff_w_qkv, diff_w_o, DIFF_SCALE, 2),
                2: (moba_w_qkv, moba_w_o, ATTN_SCALE, 1),
                3: (swa_w_qkv, swa_w_o, ATTN_SCALE, 1)}[kind]
            w_slots, w_kv, wq = _gqa_weights(w_qkv[l], scale)
            w_qdec = _decode_q_weights(wq, nmap)
            w_ob = w_o[l].astype(BF16)
            w_odec = _decode_o_weights(w_o[l])
            slots, kv = _proj(hp, [w_slots, w_kv], [BF16, F32], g=g_mix, name="qkv_p%d" % kind)
            qdec, kv_s = _proj(hs, [w_qdec, w_kv], [BF16, F32], g=g_mix, name="qkv_s%d" % kind)
            k_p = kv[:, :kw].reshape(b, s, N_KV, HEAD_DIM)
            v_p = kv[:, kw:].reshape(b, s, N_KV, HEAD_DIM)
            k_s, v_s = kv_s[:, :kw], kv_s[:, kw:]
            if kind == 1:
                lam_init = 0.8 - 0.6 * math.exp(-0.3 * i)
                lamv = _pad_lanes(diff_lambda[l].astype(F32))
                gh = _pad_lanes(diff_g_head[l].astype(F32).reshape(1, -1))
                o = _gqa_prompt_attn(slots, bias_causal, b, s, "diff", lamv, gh, lam_init)
                gh4 = jnp.tile(diff_g_head[l].astype(F32).reshape(1, -1), (1, N_KV))
                o_s = _gqa_decode_attn(page_table, qdec, k_s, v_s, jnp.concatenate([bdec, bdec], axis=0),
                                       jnp.concatenate([bself, bself], axis=0), cache_diff_k, cache_diff_v, l,
                                       "diff", lamv, gh4, lam_init)
                pk, pv, sk, sv = "dk_p", "dv_p", "dk_s", "dv_s"
            elif kind == 2:
                o = _gqa_prompt_attn(slots, bias_causal, b, s, "moba")
                o_s = _gqa_decode_attn(page_table, qdec, k_s, v_s, bdec, bself, cache_moba_k, cache_moba_v, l, "moba")
                pk, pv, sk, sv = "mk_p", "mv_p", "mk_s", "mv_s"
            else:
                o = _swa_prompt_attn(slots, bias_swa, swa_sinks[l], b, s)
                sink_rows = jnp.broadcast_to(swa_sinks[l].astype(F32)[:, None], (N_HEADS, LANE))
                o_s = _swa_decode_attn(qdec, state_swa_k[l], state_swa_v[l], k_s, v_s, bdec_swa, bself_swa, sink_rows)
            (hp,) = _proj(o.reshape(b * s, -1), [w_ob], [F32], res=hp, name="attn_out_p%d" % kind)
            (hs,) = _proj(o_s.reshape(db, -1), [w_odec], [F32], res=hs, name="attn_out_s%d" % kind)
            if kind == 3:
                w_buf = state_swa_k.shape[2]
                outs["sk_p"].append(k_p[:, s - w_buf:])
                outs["sv_p"].append(v_p[:, s - w_buf:])
                outs["sk_s"].append(jnp.concatenate([state_swa_k[l], k_s.reshape(db, 1, N_KV, HEAD_DIM)], axis=1)[:, 1:])
                outs["sv_s"].append(jnp.concatenate([state_swa_v[l], v_s.reshape(db, 1, N_KV, HEAD_DIM)], axis=1)[:, 1:])
            else:
                outs[pk].append(k_p.reshape(b, np_pages, PAGE, N_KV, HEAD_DIM))
                outs[pv].append(v_p.reshape(b, np_pages, PAGE, N_KV, HEAD_DIM))
                outs[sk].append(k_s.reshape(db, 1, N_KV, HEAD_DIM))
                outs[sv].append(v_s.reshape(db, 1, N_KV, HEAD_DIM))

        w_up = ffn_w_up[i].astype(BF16)
        w_down = ffn_w_down[i].astype(BF16)
        hp, conv_p = _ffn_prompt(hp, norm_ffn_g[i], w_up, ffn_conv_w[i], ffn_conv_b[i], w_down, b, s)
        hs, conv_s = _ffn_sample(hs, norm_ffn_g[i], w_up, ffn_conv_w[i], ffn_conv_b[i], w_down, state_ffn_conv[i])
        outs["conv_p"].append(conv_p)
        outs["conv_s"].append(conv_s)

    y_prompt = _final_norm(hp, norm_final_g).reshape(b, s, d)
    y_sample = _final_norm(hs, norm_final_g).reshape(db, 1, d)
    st = lambda k: jnp.stack(outs[k])
    return (y_prompt, y_sample,
            st("lat_p"), st("lat_s"), st("kpe_p"), st("kpe_s"),
            st("dk_p"), st("dk_s"), st("dv_p"), st("dv_s"),
            st("mk_p"), st("mk_s"), st("mv_p"), st("mv_s"),
            st("sk_p"), st("sk_s"), st("sv_p"), st("sv_s"),
            st("conv_p"), st("conv_s"))
```

```python
import functools
import math

import jax
import jax.numpy as jnp
import numpy as np
from jax import lax
from jax.experimental import pallas as pl
from jax.experimental.pallas import tpu as pltpu

F32 = jnp.float32
BF16 = jnp.bfloat16

LANE = 128
HEAD_DIM = 64
N_HEADS = 16
N_KV = 4
GROUP = 4
PAGE = 128
MLA_Q_LORA = 384
MLA_KV_LORA = 256
MLA_NOPE = 64
MLA_ROPE = 32
LOG2E = 1.4426950408889634
MLA_SCALE = (MLA_NOPE + MLA_ROPE) ** -0.5
ATTN_SCALE = HEAD_DIM ** -0.5
DIFF_QK = 32
DIFF_SCALE = DIFF_QK ** -0.5
ROPE_THETA = 10000.0
MOBA_BLOCK = 256
MOBA_TOPK = 3
N_BUCKETS = 32
EPS = 1e-6
NEG = -1e30
ONES_LANE = HEAD_DIM
CHUNK = 512
PAGE_UNROLL = 8
FFN_ROWS = 128
VMEM_LIMIT = 48 * 1024 * 1024

T_DIAG, T_PREV, T_SWA_DIAG, T_SWA_PREV, T_DEC, T_SWA_DEC = range(6)
_SHIFTED_TILES = (T_DIAG, T_PREV, T_DEC)


def _cparams(sem, vmem=None):
    return pltpu.CompilerParams(dimension_semantics=sem, vmem_limit_bytes=vmem)


def _nt_dot(a, b):
    return lax.dot_general(a, b, (((1,), (1,)), ((), ())), preferred_element_type=F32)


def _rms(x, g):
    return x * lax.rsqrt(jnp.mean(x * x, axis=-1, keepdims=True) + EPS) * g


def _bucket_np(dist):
    n = np.maximum(dist, 0)
    nf = np.maximum(n, 1).astype(np.float32)
    large = 16 + (np.log(nf / np.float32(16)) / np.float32(math.log(8.0)) * np.float32(16)).astype(np.int32)
    large = np.minimum(large, N_BUCKETS - 1)
    return np.where(n < 16, n, large).astype(np.int32)


def _bucket_tiles():
    i = np.arange(LANE)[:, None]
    c = np.arange(LANE)[None, :]
    diag = np.where(i >= c, _bucket_np(i - c), -1)
    prev = _bucket_np(LANE + i - c)
    swa_prev = np.where(c >= i, _bucket_np(LANE + i - c), -1)
    dec = _bucket_np(LANE - c + 0 * i)
    return np.stack([diag, prev, diag, swa_prev, dec, dec]).astype(np.int32)


def _bias_tiles_body(rb_ref, d_ref, o_ref):
    h = pl.program_id(0)
    far = rb_ref[N_BUCKETS - 1, h]
    for t in range(6):
        d = d_ref[t]
        acc = jnp.zeros((LANE, LANE), F32)
        for b in range(N_BUCKETS):
            acc = jnp.where(d == b, rb_ref[b, h], acc)
        if t in _SHIFTED_TILES:
            acc = acc - far
        o_ref[t, 0] = jnp.where(d < 0, NEG, acc * LOG2E)


def _bias_tiles(rel_bias):
    d = jnp.asarray(_bucket_tiles())
    return pl.pallas_call(
        _bias_tiles_body,
        grid=(N_HEADS,),
        in_specs=[pl.BlockSpec(memory_space=pltpu.SMEM),
                  pl.BlockSpec((6, LANE, LANE), lambda h: (0, 0, 0))],
        out_specs=pl.BlockSpec((6, 1, LANE, LANE), lambda h: (0, h, 0, 0)),
        out_shape=jax.ShapeDtypeStruct((6, N_HEADS, LANE, LANE), F32),
        compiler_params=_cparams(("arbitrary",)),
        name="bias_tiles",
    )(rel_bias.astype(F32), d)


def _proj_body(*refs, n_w, norm, residual):
    it = iter(refs)
    x_ref = next(it)
    g_ref = next(it) if norm else None
    r_ref = next(it) if residual else None
    w_refs = [next(it) for _ in range(n_w)]
    o_refs = [next(it) for _ in range(n_w)]
    x = x_ref[...]
    if norm:
        x = _rms(x.astype(F32), g_ref[...])
    xb = x.astype(BF16)
    for k in range(n_w):
        acc = jnp.dot(xb, w_refs[k][...], preferred_element_type=F32)
        if residual and k == 0:
            acc = acc + r_ref[...]
        o_refs[k][...] = acc.astype(o_refs[k].dtype)


def _proj(x, ws, out_dtypes, g=None, res=None, tm=512, name="proj"):
    t, kdim = x.shape
    tm = min(tm, t)
    args = [x]
    specs = [pl.BlockSpec((tm, kdim), lambda i: (i, 0))]
    if g is not None:
        args.append(g.reshape(1, kdim).astype(F32))
        specs.append(pl.BlockSpec((1, kdim), lambda i: (0, 0)))
    if res is not None:
        args.append(res)
        specs.append(pl.BlockSpec((tm, res.shape[1]), lambda i: (i, 0)))
    for w in ws:
        args.append(w)
        specs.append(pl.BlockSpec(w.shape, lambda i: (0, 0)))
    outs = pl.pallas_call(
        functools.partial(_proj_body, n_w=len(ws), norm=g is not None, residual=res is not None),
        grid=(t // tm,),
        in_specs=specs,
        out_specs=[pl.BlockSpec((tm, w.shape[1]), lambda i: (i, 0)) for w in ws],
        out_shape=[jax.ShapeDtypeStruct((t, w.shape[1]), dt) for w, dt in zip(ws, out_dtypes)],
        compiler_params=_cparams(("arbitrary",), VMEM_LIMIT),
        name=name,
    )(*args)
    return outs


def _norm_body(x_ref, g_ref, o_ref):
    o_ref[...] = _rms(x_ref[...], g_ref[...])


def _final_norm(x, g, tm=512):
    t, d = x.shape
    tm = min(tm, t)
    return pl.pallas_call(
        _norm_body,
        grid=(t // tm,),
        in_specs=[pl.BlockSpec((tm, d), lambda i: (i, 0)), pl.BlockSpec((1, d), lambda i: (0, 0))],
        out_specs=pl.BlockSpec((tm, d), lambda i: (i, 0)),
        out_shape=jax.ShapeDtypeStruct((t, d), F32),
        compiler_params=_cparams(("arbitrary",)),
        name="final_norm",
    )(x, g.reshape(1, d))


def _headwise_body(x_ref, w_ref, o_ref):
    o_ref[...] = jnp.dot(x_ref[...].astype(BF16), w_ref[0], preferred_element_type=F32).astype(o_ref.dtype)


def _headwise(x, w, out_dtype, name):
    m = x.shape[0]
    n, kb, nb = w.shape
    return pl.pallas_call(
        _headwise_body,
        grid=(n,),
        in_specs=[pl.BlockSpec((m, kb), lambda i: (0, i)), pl.BlockSpec((1, kb, nb), lambda i: (i, 0, 0))],
        out_specs=pl.BlockSpec((m, nb), lambda i: (0, i)),
        out_shape=jax.ShapeDtypeStruct((m, n * nb), out_dtype),
        compiler_params=_cparams(("arbitrary",)),
        name=name,
    )(x, w)


def _mla_stage2_body(s1_ref, gq_ref, gkv_ref, cq_ref, sq_ref, ck_ref, wuq_ref, wuk_ref, wuv_ref,
                     q_ref, k_ref, v_ref, lat_ref, kpe_ref):
    s1 = s1_ref[...]
    cosq, sinq, cosk = cq_ref[...], sq_ref[...], ck_ref[...]
    cqn = _rms(s1[:, :MLA_Q_LORA], gq_ref[...]).astype(BF16)
    qa = jnp.dot(cqn, wuq_ref[...], preferred_element_type=F32)
    for h in range(N_HEADS):
        blk = qa[:, h * LANE:(h + 1) * LANE]
        q_ref[:, h * LANE:(h + 1) * LANE] = (blk * cosq + pltpu.roll(blk, LANE - MLA_ROPE, axis=1) * sinq).astype(BF16)
    lat = _rms(s1[:, MLA_Q_LORA:MLA_Q_LORA + MLA_KV_LORA], gkv_ref[...])
    lat_ref[...] = lat
    latb = lat.astype(BF16)
    pe = s1[:, MLA_Q_LORA + MLA_KV_LORA:]
    kpe = pltpu.roll(pe, 2 * MLA_ROPE, axis=1) * cosk + pltpu.roll(pe, MLA_ROPE, axis=1) * sinq
    kpe_ref[...] = kpe
    kn = jnp.dot(latb, wuk_ref[...], preferred_element_type=F32)
    vn = jnp.dot(latb, wuv_ref[...], preferred_element_type=F32)
    lane = lax.broadcasted_iota(jnp.int32, kpe.shape, 1)
    ones = jnp.where(lane == ONES_LANE, 1.0, 0.0)
    for h in range(N_HEADS):
        k_ref[:, h * LANE:(h + 1) * LANE] = (kn[:, h * LANE:(h + 1) * LANE] + kpe).astype(BF16)
        v_ref[:, h * LANE:(h + 1) * LANE] = (vn[:, h * LANE:(h + 1) * LANE] + ones).astype(BF16)


def _mla_stage2(s1, g_q, g_kv, cosq, sinq, cosk, wuq, wuk, wuv, tm=256):
    t = s1.shape[0]
    tm = min(tm, t)
    ntab = cosq.shape[0] // tm
    row = lambda i: (i, 0)
    tab = lambda i: (i % ntab, 0)
    full = lambda i: (0, 0)
    hw = N_HEADS * LANE
    return pl.pallas_call(
        _mla_stage2_body,
        grid=(t // tm,),
        in_specs=[pl.BlockSpec((tm, s1.shape[1]), row),
                  pl.BlockSpec((1, MLA_Q_LORA), full), pl.BlockSpec((1, MLA_KV_LORA), full),
                  pl.BlockSpec((tm, LANE), tab), pl.BlockSpec((tm, LANE), tab), pl.BlockSpec((tm, LANE), tab),
                  pl.BlockSpec(wuq.shape, full), pl.BlockSpec(wuk.shape, full), pl.BlockSpec(wuv.shape, full)],
        out_specs=[pl.BlockSpec((tm, hw), row), pl.BlockSpec((tm, hw), row), pl.BlockSpec((tm, hw), row),
                   pl.BlockSpec((tm, MLA_KV_LORA), row), pl.BlockSpec((tm, LANE), row)],
        out_shape=[jax.ShapeDtypeStruct((t, hw), BF16), jax.ShapeDtypeStruct((t, hw), BF16),
                   jax.ShapeDtypeStruct((t, hw), BF16),
                   jax.ShapeDtypeStruct((t, MLA_KV_LORA), F32), jax.ShapeDtypeStruct((t, LANE), F32)],
        compiler_params=_cparams(("arbitrary",), VMEM_LIMIT),
        name="mla_stage2",
    )(s1, g_q.reshape(1, -1), g_kv.reshape(1, -1), cosq, sinq, cosk, wuq, wuk, wuv)


QK_LOOKAHEAD = 4


def _flash_scores(qs, ks, s_wr):
    r = qs[0].shape[0]
    for c, q in enumerate(qs):
        s_wr[c * r:(c + 1) * r, :] = _nt_dot(q, ks[c])


def _flash_stage(qs, ks_next, vs, s_sc, m_sc, acc_sc, biases=None):
    n = len(qs)
    r = qs[0].shape[0]
    pending = {}

    def scores(c):
        if ks_next is not None and c < n:
            pending[c] = _nt_dot(qs[c], ks_next[c])

    m_all, acc_all = m_sc[...], acc_sc[...]
    new_m, new_acc = [], []
    for c in range(QK_LOOKAHEAD):
        scores(c)
    for c in range(n):
        s = s_sc[c * r:(c + 1) * r, :]
        if c in pending:
            s_sc[c * r:(c + 1) * r, :] = pending.pop(c)
        if biases is not None:
            s = s + biases[c]
        cols = [s[:, w * LANE:(w + 1) * LANE] for w in range(s.shape[1] // LANE)]
        mx = cols[0]
        for col in cols[1:]:
            mx = jnp.maximum(mx, col)
        m_prev = m_all[c * r:(c + 1) * r]
        m_new = jnp.maximum(m_prev, jnp.max(mx, axis=1, keepdims=True))
        alpha = jnp.exp2(m_prev - m_new)
        p = jnp.concatenate([jnp.exp2(col - m_new) for col in cols], axis=1).astype(BF16)
        new_acc.append(alpha * acc_all[c * r:(c + 1) * r] + jnp.dot(p, vs[c], preferred_element_type=F32))
        new_m.append(m_new)
        scores(c + QK_LOOKAHEAD)
    m_sc[...] = jnp.concatenate(new_m, axis=0)
    acc_sc[...] = jnp.concatenate(new_acc, axis=0)


def _init_flash(m_sc, acc_sc):
    m_sc[...] = jnp.full(m_sc.shape, NEG, F32)
    acc_sc[...] = jnp.zeros(acc_sc.shape, F32)


def _normalised(acc):
    return acc / acc[:, ONES_LANE:ONES_LANE + 1]


def _mla_prompt_body(q_ref, k_ref, v_ref, o_ref, m_sc, acc_sc, s_sc, *, tq, r, nh):
    qi = pl.program_id(2)
    nrb = tq // r
    qs = [q_ref[0, rb * r:(rb + 1) * r, hh * LANE:(hh + 1) * LANE] for hh in range(nh) for rb in range(nrb)]
    _init_flash(m_sc, acc_sc)

    def chunk(ref, start):
        return [ref[0, pl.ds(start, tq), hh * LANE:(hh + 1) * LANE] for hh in range(nh) for _ in range(nrb)]

    _flash_scores(qs, chunk(k_ref, 0), s_sc)

    def far(j, carry):
        _flash_stage(qs, chunk(k_ref, pl.multiple_of((j + 1) * tq, tq)), chunk(v_ref, pl.multiple_of(j * tq, tq)),
                     s_sc, m_sc, acc_sc)
        return carry

    lax.fori_loop(0, qi, far, 0)
    row = lax.broadcasted_iota(jnp.int32, (r, tq), 0)
    col = lax.broadcasted_iota(jnp.int32, (r, tq), 1)
    causal = [jnp.where(col <= row + rb * r, 0.0, NEG) for rb in range(nrb)] * nh
    _flash_stage(qs, None, chunk(v_ref, pl.multiple_of(qi * tq, tq)), s_sc, m_sc, acc_sc, causal)
    lane = lax.broadcasted_iota(jnp.int32, (r, LANE), 1)
    for pr in range(nh // 2):
        for rb in range(nrb):
            left = _normalised(acc_sc[(2 * pr * nrb + rb) * r:(2 * pr * nrb + rb + 1) * r])
            right = _normalised(acc_sc[((2 * pr + 1) * nrb + rb) * r:((2 * pr + 1) * nrb + rb + 1) * r])
            o_ref[0, rb * r:(rb + 1) * r, pr * LANE:(pr + 1) * LANE] = jnp.where(
                lane < HEAD_DIM, left, pltpu.roll(right, HEAD_DIM, axis=1)).astype(o_ref.dtype)


def _mla_prompt_attn(qs, ks, vs, b, s, tq=CHUNK, r=128, nh=4):
    tq = min(tq, s)
    hw = N_HEADS * LANE
    qs, ks, vs = qs.reshape(b, s, hw), ks.reshape(b, s, hw), vs.reshape(b, s, hw)
    nch = nh * (tq // r)
    return pl.pallas_call(
        functools.partial(_mla_prompt_body, tq=tq, r=r, nh=nh),
        grid=(b, N_HEADS // nh, s // tq),
        in_specs=[pl.BlockSpec((1, tq, nh * LANE), lambda bi, hp, qi: (bi, qi, hp)),
                  pl.BlockSpec((1, s, nh * LANE), lambda bi, hp, qi: (bi, 0, hp)),
                  pl.BlockSpec((1, s, nh * LANE), lambda bi, hp, qi: (bi, 0, hp))],
        out_specs=pl.BlockSpec((1, tq, nh * HEAD_DIM), lambda bi, hp, qi: (bi, qi, hp)),
        out_shape=jax.ShapeDtypeStruct((b, s, N_HEADS * HEAD_DIM), BF16),
        scratch_shapes=[pltpu.VMEM((nch * r, LANE), F32)] * 2 + [pltpu.VMEM((nch * r, tq), F32)],
        compiler_params=_cparams(("arbitrary", "arbitrary", "arbitrary"), VMEM_LIMIT),
        name="mla_prompt_attn",
    )(qs, ks, vs)


def _lambda_value(lamv, lam_init):
    a = jnp.sum(lamv[0:1] * lamv[1:2], axis=1, keepdims=True)
    c = jnp.sum(lamv[2:3] * lamv[3:4], axis=1, keepdims=True)
    return jnp.exp(a) - jnp.exp(c) + lam_init


def _with_ones(v):
    lane = lax.broadcasted_iota(jnp.int32, v.shape, 1)
    return jnp.where(lane == ONES_LANE, 1.0, v.astype(F32)).astype(BF16)


def _gqa_prompt_body(*refs, kind, tq, w, lam_init):
    if kind == "diff":
        q_ref, k_ref, v_ref, bias_ref, lamv_ref, gh_ref, o_ref, q_sc, m_sc, acc_sc, s_sc, v_sc = refs
        nmap = 2
    else:
        q_ref, k_ref, v_ref, bias_ref, o_ref, q_sc, m_sc, acc_sc, s_sc, v_sc, k_sc, kmh_sc, kml_sc = refs
        nmap = 1
    qi = pl.program_id(2)
    r = LANE
    nrb = tq // r
    nch = nmap * GROUP * nrb
    s_len = k_ref.shape[1]
    bpc = w // r
    lane_t = lax.broadcasted_iota(jnp.int32, (r, LANE), 1)

    def q_block(hh, rb):
        return q_ref[0, rb * r:(rb + 1) * r, hh * LANE:(hh + 1) * LANE]

    @pl.when(qi == 0)
    def _():
        for n in range(s_len // MOBA_BLOCK):
            sl = slice(n * MOBA_BLOCK, (n + 1) * MOBA_BLOCK)
            v_sc[sl, :] = _with_ones(v_ref[0, sl, :])
            if kind == "moba":
                kb = k_ref[0, sl, :].astype(F32)
                lane_k = lax.broadcasted_iota(jnp.int32, kb.shape, 1)
                k_sc[sl, :] = jnp.where(lane_k == HEAD_DIM + n, 1.0, kb).astype(BF16)
                if n == 0:
                    kmh_sc[...] = jnp.zeros(kmh_sc.shape, BF16)
                    kml_sc[...] = jnp.zeros(kml_sc.shape, BF16)
                km = jnp.sum(kb, axis=0, keepdims=True) * (1.0 / MOBA_BLOCK)
                hi = km.astype(BF16)
                kmh_sc[HEAD_DIM + n:HEAD_DIM + n + 1, :] = hi
                kml_sc[HEAD_DIM + n:HEAD_DIM + n + 1, :] = (km - hi.astype(F32)).astype(BF16)

    if kind == "diff":
        for mp in range(2):
            keep = (lane_t >= mp * DIFF_QK) & (lane_t < (mp + 1) * DIFF_QK)
            for hh in range(GROUP):
                for rb in range(nrb):
                    r0 = ((mp * GROUP + hh) * nrb + rb) * r
                    q_sc[r0:r0 + r, :] = jnp.where(keep, q_block(hh, rb).astype(F32), 0.0).astype(BF16)
    else:
        nblk = s_len // MOBA_BLOCK
        rows = GROUP * tq
        q0 = jnp.concatenate([q_block(hh, rb) for hh in range(GROUP) for rb in range(nrb)], axis=0)
        gate = _nt_dot(q0, kmh_sc[...]) + _nt_dot(q0, kml_sc[...])
        lane_r = lax.broadcasted_iota(jnp.int32, (rows, LANE), 1)
        lane_f = lane_r.astype(F32)
        q_blk = (qi * tq) // MOBA_BLOCK
        elig = (lane_r >= HEAD_DIM) & (lane_r < HEAD_DIM + q_blk)
        gsel = jnp.where(elig, gate, NEG)
        pen = jnp.where((lane_r >= HEAD_DIM) & (lane_r < HEAD_DIM + nblk), NEG, 0.0)
        for _ in range(MOBA_TOPK):
            mx = jnp.max(gsel, axis=1, keepdims=True)
            idx = jnp.min(jnp.where(gsel == mx, lane_f, 1e9), axis=1, keepdims=True)
            pick = (lane_f == idx) & (mx > 0.5 * NEG)
            pen = jnp.where(pick, 0.0, pen)
            gsel = jnp.where(pick, NEG, gsel)
        pen = jnp.where(lane_r == HEAD_DIM + q_blk, 0.0, pen)
        q_sc[...] = (q0.astype(F32) + pen).astype(BF16)

    qs = [q_sc[c * r:(c + 1) * r, :] for c in range(nch)]
    _init_flash(m_sc, acc_sc)
    kk_ref = k_sc if kind == "moba" else k_ref.at[0]

    def chunk_bias(j):
        per_block = {}
        for hh in range(GROUP):
            diag = bias_ref[0, 0, hh * r:(hh + 1) * r, :]
            prev = bias_ref[1, 0, hh * r:(hh + 1) * r, :]
            for rb in range(nrb):
                rel0 = j * bpc - (qi * nrb + rb)
                per_block[hh, rb] = jnp.concatenate(
                    [jnp.where(rel0 + t == 0, diag, jnp.where(rel0 + t == -1, prev, jnp.where(rel0 + t > 0, NEG, 0.0)))
                     for t in range(bpc)], axis=1)
        return [per_block[hh, rb] for _ in range(nmap) for hh in range(GROUP) for rb in range(nrb)]

    def stage(j, has_next, biased):
        ks = [kk_ref[pl.ds(pl.multiple_of((j + 1) * w, w), w), :]] * nch if has_next else None
        v = v_sc[pl.ds(pl.multiple_of(j * w, w), w), :]
        _flash_stage(qs, ks, [v] * nch, s_sc, m_sc, acc_sc, chunk_bias(j) if biased else None)

    qb0 = qi * nrb
    jl = (qb0 + nrb - 1) // bpc
    prev_in_earlier = jnp.logical_and(jl >= 1, jl * bpc >= qb0)
    n_plain = jnp.where(prev_in_earlier, jl - 1, jl)

    _flash_scores(qs, [kk_ref[0:w, :]] * nch, s_sc)

    def plain(j, carry):
        stage(j, True, False)
        return carry

    lax.fori_loop(0, n_plain, plain, 0)

    @pl.when(prev_in_earlier)
    def _():
        stage(jl - 1, True, True)

    stage(jl, False, True)

    def chain_rows(ref, mp, hh, rb):
        c = (mp * GROUP + hh) * nrb + rb
        return ref[c * r:(c + 1) * r]

    for rb in range(nrb):
        heads = []
        for hh in range(GROUP):
            if kind == "diff":
                lam = _lambda_value(lamv_ref[...], lam_init)
                a = _normalised(chain_rows(acc_sc, 0, hh, rb)) - lam * _normalised(chain_rows(acc_sc, 1, hh, rb))
                a = jnp.where(lane_t < HEAD_DIM, a, 0.0)
                ms = jnp.sum(a * a, axis=1, keepdims=True) * (1.0 / HEAD_DIM)
                heads.append(a * lax.rsqrt(ms + EPS) * gh_ref[...] * (1.0 - lam_init))
            else:
                heads.append(_normalised(chain_rows(acc_sc, 0, hh, rb)))
        for pr in range(GROUP // 2):
            lane = lax.broadcasted_iota(jnp.int32, (r, LANE), 1)
            o_ref[0, rb * r:(rb + 1) * r, pr * LANE:(pr + 1) * LANE] = jnp.where(
                lane < HEAD_DIM, heads[2 * pr], pltpu.roll(heads[2 * pr + 1], HEAD_DIM, axis=1)).astype(o_ref.dtype)


def _gqa_prompt_attn(slots, bias, b, s, kind, lamv=None, gh=None, lam_init=0.0, tq=256, w=CHUNK):
    w = min(w, s)
    assert s % w == 0 and w % tq == 0 and tq % LANE == 0 and s % MOBA_BLOCK == 0 and s // MOBA_BLOCK <= 32
    assert MOBA_BLOCK % tq == 0
    nslot = slots.shape[1] // LANE
    slots = slots.reshape(b, s, nslot * LANE)
    nmap = 2 if kind == "diff" else 1
    rows = nmap * GROUP * tq
    args = [slots, slots, slots, bias]
    specs = [pl.BlockSpec((1, tq, GROUP * LANE), lambda bi, g, qi: (bi, qi, g)),
             pl.BlockSpec((1, s, LANE), lambda bi, g, qi: (bi, 0, N_HEADS + g)),
             pl.BlockSpec((1, s, LANE), lambda bi, g, qi: (bi, 0, N_HEADS + N_KV + g)),
             pl.BlockSpec((2, 1, GROUP * LANE, LANE), lambda bi, g, qi: (0, g, 0, 0))]
    scratch = [pltpu.VMEM((rows, LANE), BF16), pltpu.VMEM((rows, LANE), F32), pltpu.VMEM((rows, LANE), F32),
               pltpu.VMEM((rows, w), F32), pltpu.VMEM((s, LANE), BF16)]
    if kind == "diff":
        args += [lamv, gh]
        specs += [pl.BlockSpec((4, LANE), lambda bi, g, qi: (0, 0)), pl.BlockSpec((1, LANE), lambda bi, g, qi: (0, 0))]
    else:
        scratch += [pltpu.VMEM((s, LANE), BF16), pltpu.VMEM((LANE, LANE), BF16), pltpu.VMEM((LANE, LANE), BF16)]
    return pl.pallas_call(
        functools.partial(_gqa_prompt_body, kind=kind, tq=tq, w=w, lam_init=lam_init),
        grid=(b, N_KV, s // tq),
        in_specs=specs,
        out_specs=pl.BlockSpec((1, tq, GROUP * HEAD_DIM), lambda bi, g, qi: (bi, qi, g)),
        out_shape=jax.ShapeDtypeStruct((b, s, N_HEADS * HEAD_DIM), BF16),
        scratch_shapes=scratch,
        compiler_params=_cparams(("arbitrary", "arbitrary", "arbitrary"), VMEM_LIMIT),
        name=kind + "_prompt_attn",
    )(*args)


def _swa_prompt_body(sink_ref, q_ref, kp_ref, kc_ref, vp_ref, vc_ref, bias_ref, o_ref, *, nq):
    g = pl.program_id(1)
    qi = pl.program_id(2)
    r = LANE
    sink = jnp.concatenate([jnp.full((r, LANE), sink_ref[g * GROUP + hh] * LOG2E, F32) for hh in range(GROUP)], axis=0)

    def tile(ref_prev, ref_cur, t):
        return ref_prev[0] if t == 0 else ref_cur[0, (t - 1) * r:t * r, :]

    scores = []
    for t in range(nq):
        q = jnp.concatenate([q_ref[0, t * r:(t + 1) * r, hh * LANE:(hh + 1) * LANE] for hh in range(GROUP)], axis=0)
        s_prev = _nt_dot(q, tile(kp_ref, kc_ref, t)) + bias_ref[1, 0]
        if t == 0:
            s_prev = jnp.where(qi == 0, NEG, s_prev)
        scores.append((s_prev, _nt_dot(q, kc_ref[0, t * r:(t + 1) * r, :]) + bias_ref[0, 0]))
    for t, (s_prev, s_cur) in enumerate(scores):
        m = jnp.maximum(jnp.max(jnp.maximum(s_prev, s_cur), axis=1, keepdims=True), sink)
        p_prev = jnp.exp2(s_prev - m)
        p_cur = jnp.exp2(s_cur - m)
        den = jnp.sum(p_prev + p_cur, axis=1, keepdims=True) + jnp.exp2(sink - m)
        o = (jnp.dot(p_prev.astype(BF16), tile(vp_ref, vc_ref, t), preferred_element_type=F32)
             + jnp.dot(p_cur.astype(BF16), vc_ref[0, t * r:(t + 1) * r, :], preferred_element_type=F32)) / den
        lane = lax.broadcasted_iota(jnp.int32, (r, LANE), 1)
        for pr in range(GROUP // 2):
            left, right = o[2 * pr * r:(2 * pr + 1) * r], o[(2 * pr + 1) * r:(2 * pr + 2) * r]
            o_ref[0, t * r:(t + 1) * r, pr * LANE:(pr + 1) * LANE] = jnp.where(
                lane < HEAD_DIM, left, pltpu.roll(right, HEAD_DIM, axis=1)).astype(o_ref.dtype)


def _swa_prompt_attn(slots, bias, sinks, b, s, nq=4):
    nslot = slots.shape[1] // LANE
    slots = slots.reshape(b, s, nslot * LANE)
    tq = nq * LANE
    assert s % tq == 0
    prev = lambda off: (lambda bi, g, qi: (bi, jnp.maximum(nq * qi - 1, 0), off + g))
    cur = lambda off: (lambda bi, g, qi: (bi, qi, off + g))
    return pl.pallas_call(
        functools.partial(_swa_prompt_body, nq=nq),
        grid=(b, N_KV, s // tq),
        in_specs=[pl.BlockSpec(memory_space=pltpu.SMEM),
                  pl.BlockSpec((1, tq, GROUP * LANE), lambda bi, g, qi: (bi, qi, g)),
                  pl.BlockSpec((1, LANE, LANE), prev(N_HEADS)), pl.BlockSpec((1, tq, LANE), cur(N_HEADS)),
                  pl.BlockSpec((1, LANE, LANE), prev(N_HEADS + N_KV)), pl.BlockSpec((1, tq, LANE), cur(N_HEADS + N_KV)),
                  pl.BlockSpec((2, 1, GROUP * LANE, LANE), lambda bi, g, qi: (0, g, 0, 0))],
        out_specs=pl.BlockSpec((1, tq, GROUP * HEAD_DIM), lambda bi, g, qi: (bi, qi, g)),
        out_shape=jax.ShapeDtypeStruct((b, s, N_HEADS * HEAD_DIM), BF16),
        compiler_params=_cparams(("arbitrary", "arbitrary", "arbitrary")),
        name="swa_prompt_attn",
    )(sinks.astype(F32), slots, slots, slots, slots, slots, bias)


def _silu(x):
    return x / (1.0 + jnp.exp(-x))


def _ffn_prompt_body(x_ref, g_ref, wg_ref, wu_ref, cwg_ref, cwu_ref, cbg_ref, cbu_ref, wd_ref,
                     o_ref, sg_ref, su_ref, xn_sc, acc_sc, carry_sc, *, tiles_per_seq):
    i = pl.program_id(0)
    j = pl.program_id(1)
    tm = x_ref.shape[0]

    @pl.when(j == 0)
    def _():
        xn_sc[...] = _rms(x_ref[...], g_ref[...]).astype(BF16)
        acc_sc[...] = jnp.zeros(acc_sc.shape, F32)

    first = (i % tiles_per_seq) == 0
    rbs = min(FFN_ROWS, tm)
    row8 = lax.broadcasted_iota(jnp.int32, (8, wg_ref.shape[1]), 0)

    @pl.when(first)
    def _():
        carry_sc[j] = jnp.zeros(carry_sc.shape[1:], F32)

    def up(rb):
        xb = xn_sc[rb * rbs:(rb + 1) * rbs, :]
        return (jnp.dot(xb, wg_ref[...], preferred_element_type=F32),
                jnp.dot(xb, wu_ref[...], preferred_element_type=F32))

    def conv(h, tail, cw_ref, cb_ref):
        p6, p7 = tail[6:7], tail[7:8]
        h1, h2 = pltpu.roll(h, 1, axis=0), pltpu.roll(h, 2, axis=0)
        h1 = jnp.concatenate([jnp.where(row8 == 0, p7, h1[:8]), h1[8:]], axis=0)
        h2 = jnp.concatenate([jnp.where(row8 == 0, p6, jnp.where(row8 == 1, p7, h2[:8])), h2[8:]], axis=0)
        return cb_ref[...] + cw_ref[0:1] * h2 + cw_ref[1:2] * h1 + cw_ref[2:3] * h

    tail_g, tail_u = carry_sc[j, 0], carry_sc[j, 1]
    pending = up(0)
    for rb in range(tm // rbs):
        hg, hu = pending
        if (rb + 1) * rbs < tm:
            pending = up(rb + 1)
        act = _silu(conv(hg, tail_g, cwg_ref, cbg_ref)) * conv(hu, tail_u, cwu_ref, cbu_ref)
        acc_sc[rb * rbs:(rb + 1) * rbs, :] += jnp.dot(act.astype(BF16), wd_ref[...], preferred_element_type=F32)
        tail_g, tail_u = hg[rbs - 8:], hu[rbs - 8:]
    carry_sc[j, 0] = tail_g
    carry_sc[j, 1] = tail_u
    sg_ref[0] = tail_g
    su_ref[0] = tail_u

    @pl.when(j == pl.num_programs(1) - 1)
    def _():
        o_ref[...] = x_ref[...] + acc_sc[...]


def _ffn_prompt(x, g, w_up, conv_w, conv_b, w_down, b, s, tm=1024, fc=256):
    t, d = x.shape
    dff = w_down.shape[0]
    tm = min(tm, s)
    nf = dff // fc
    row = lambda i, j: (i, 0)
    gcol = lambda i, j: (0, j)
    ucol = lambda i, j: (0, nf + j)
    tps = s // tm
    out, sg, su = pl.pallas_call(
        functools.partial(_ffn_prompt_body, tiles_per_seq=tps),
        grid=(t // tm, nf),
        in_specs=[pl.BlockSpec((tm, d), row), pl.BlockSpec((1, d), lambda i, j: (0, 0)),
                  pl.BlockSpec((d, fc), gcol), pl.BlockSpec((d, fc), ucol),
                  pl.BlockSpec((3, fc), gcol), pl.BlockSpec((3, fc), ucol),
                  pl.BlockSpec((1, fc), gcol), pl.BlockSpec((1, fc), ucol),
                  pl.BlockSpec((fc, d), lambda i, j: (j, 0))],
        out_specs=[pl.BlockSpec((tm, d), row),
                   pl.BlockSpec((1, 8, fc), lambda i, j: (i, 0, j)),
                   pl.BlockSpec((1, 8, fc), lambda i, j: (i, 0, j))],
        out_shape=[jax.ShapeDtypeStruct((t, d), F32), jax.ShapeDtypeStruct((t // tm, 8, dff), F32),
                   jax.ShapeDtypeStruct((t // tm, 8, dff), F32)],
        scratch_shapes=[pltpu.VMEM((tm, d), BF16), pltpu.VMEM((tm, d), F32), pltpu.VMEM((nf, 2, 8, fc), F32)],
        compiler_params=_cparams(("arbitrary", "arbitrary"), VMEM_LIMIT),
        name="ffn_prompt",
    )(x, g.reshape(1, d), w_up, w_up, conv_w, conv_w, conv_b.reshape(1, -1), conv_b.reshape(1, -1), w_down)
    last = slice(tps - 1, None, tps)
    state = jnp.concatenate([sg[last, 6:8], su[last, 6:8]], axis=-1)
    return out, state


def _ffn_sample_body(x_ref, g_ref, wg_ref, wu_ref, cwg_ref, cwu_ref, cbg_ref, cbu_ref, wd_ref,
                     s0g_ref, s1g_ref, s0u_ref, s1u_ref, o_ref, hg_ref, hu_ref, xn_sc, acc_sc):
    j = pl.program_id(0)

    @pl.when(j == 0)
    def _():
        xn_sc[...] = _rms(x_ref[...], g_ref[...]).astype(BF16)
        acc_sc[...] = jnp.zeros(acc_sc.shape, F32)

    xb = xn_sc[...]
    hg = jnp.dot(xb, wg_ref[...], preferred_element_type=F32)
    hu = jnp.dot(xb, wu_ref[...], preferred_element_type=F32)
    hg_ref[...] = hg
    hu_ref[...] = hu
    cg = cbg_ref[...] + cwg_ref[0:1] * s0g_ref[...] + cwg_ref[1:2] * s1g_ref[...] + cwg_ref[2:3] * hg
    cu = cbu_ref[...] + cwu_ref[0:1] * s0u_ref[...] + cwu_ref[1:2] * s1u_ref[...] + cwu_ref[2:3] * hu
    acc_sc[...] += jnp.dot((_silu(cg) * cu).astype(BF16), wd_ref[...], preferred_element_type=F32)

    @pl.when(j == pl.num_programs(0) - 1)
    def _():
        o_ref[...] = x_ref[...] + acc_sc[...]


def _ffn_sample(x, g, w_up, conv_w, conv_b, w_down, state, fc=256):
    t, d = x.shape
    dff = w_down.shape[0]
    nf = dff // fc
    s0, s1 = state[:, 0], state[:, 1]
    full = lambda j: (0, 0)
    gcol = lambda j: (0, j)
    ucol = lambda j: (0, nf + j)
    out, hg, hu = pl.pallas_call(
        _ffn_sample_body,
        grid=(nf,),
        in_specs=[pl.BlockSpec((t, d), full), pl.BlockSpec((1, d), full),
                  pl.BlockSpec((d, fc), gcol), pl.BlockSpec((d, fc), ucol),
                  pl.BlockSpec((3, fc), gcol), pl.BlockSpec((3, fc), ucol),
                  pl.BlockSpec((1, fc), gcol), pl.BlockSpec((1, fc), ucol),
                  pl.BlockSpec((fc, d), lambda j: (j, 0)),
                  pl.BlockSpec((t, fc), gcol), pl.BlockSpec((t, fc), gcol),
                  pl.BlockSpec((t, fc), ucol), pl.BlockSpec((t, fc), ucol)],
        out_specs=[pl.BlockSpec((t, d), full), pl.BlockSpec((t, fc), gcol), pl.BlockSpec((t, fc), gcol)],
        out_shape=[jax.ShapeDtypeStruct((t, d), F32), jax.ShapeDtypeStruct((t, dff), F32),
                   jax.ShapeDtypeStruct((t, dff), F32)],
        scratch_shapes=[pltpu.VMEM((t, d), BF16), pltpu.VMEM((t, d), F32)],
        compiler_params=_cparams(("arbitrary",), VMEM_LIMIT),
        name="ffn_sample",
    )(x, g.reshape(1, d), w_up, w_up, conv_w, conv_w, conv_b.reshape(1, -1), conv_b.reshape(1, -1), w_down,
      s0, s1, s0, s1)
    h = jnp.concatenate([hg, hu], axis=-1)
    return out, jnp.stack([s1, h], axis=1)


def _page_copies(pt_ref, seq, slot, n_pages, page_off, pairs):
    def make(p, cache, buf, sem):
        pg = pt_ref[seq, p] + page_off
        return pltpu.make_async_copy(cache.at[pg], buf.at[slot, p], sem.at[slot])

    def start(p, c):
        for cache, buf, sem in pairs:
            make(p, cache, buf, sem).start()
        return c

    def wait(p, c):
        for cache, buf, sem in pairs:
            make(p, cache, buf, sem).wait()
        return c

    return (lambda: lax.fori_loop(0, n_pages, start, 0, unroll=PAGE_UNROLL)), (
        lambda: lax.fori_loop(0, n_pages, wait, 0, unroll=PAGE_UNROLL))


def _prefetch_pages(pt_ref, n_pages, page_off, pairs):
    b = pl.program_id(0)
    nb = pl.num_programs(0)

    @pl.when(b == 0)
    def _():
        _page_copies(pt_ref, 0, 0, n_pages, page_off, pairs)[0]()

    @pl.when(b + 1 < nb)
    def _():
        _page_copies(pt_ref, b + 1, (b + 1) % 2, n_pages, page_off, pairs)[0]()

    slot = b % 2
    _page_copies(pt_ref, b, slot, n_pages, page_off, pairs)[1]()
    return slot


def _softmax_pages(s_sc, p_sc, s_self, n_pages):
    s = s_sc[...]
    m = jnp.maximum(jnp.max(jnp.max(s, axis=0), axis=1, keepdims=True), s_self)
    p = jnp.exp2(s - m[None])
    p_self = jnp.exp2(s_self - m)
    p_sc[...] = p.astype(BF16)
    return p_self, jnp.sum(jnp.sum(p, axis=0), axis=1, keepdims=True) + p_self


def _gqa_decode_body(pt_ref, q_ref, knew_ref, vnew_ref, bdec_ref, bself_ref, *rest,
                     kind, n_pages, page_off, lam_init):
    if kind == "diff":
        lamv_ref, gh_ref, kc_hbm, vc_hbm, o_ref, kbuf, vbuf, s_sc, p_sc, ksem, vsem = rest
    else:
        kc_hbm, vc_hbm, o_ref, kbuf, vbuf, s_sc, p_sc, ksem, vsem = rest
    slot = _prefetch_pages(pt_ref, n_pages, page_off, [(kc_hbm, kbuf, ksem), (vc_hbm, vbuf, vsem)])
    q = q_ref[0]
    rows, kw = q.shape

    if kind == "moba":
        ppb = MOBA_BLOCK // PAGE
        nblk = n_pages // ppb
        lane_k = lax.broadcasted_iota(jnp.int32, (kw, LANE), 1)
        kmt = jnp.zeros((kw, LANE), F32)
        for n in range(nblk):
            blk = kbuf[slot, ppb * n]
            for e in range(1, ppb):
                blk = blk + kbuf[slot, ppb * n + e]
            kmt = jnp.where(lane_k == n, jnp.sum(blk, axis=1, keepdims=True) * (1.0 / MOBA_BLOCK), kmt)
        hi = kmt.astype(BF16)
        lo = (kmt - hi.astype(F32)).astype(BF16)
        gate = jnp.dot(q, hi, preferred_element_type=F32) + jnp.dot(q, lo, preferred_element_type=F32)
        lane_r = lax.broadcasted_iota(jnp.int32, (rows, LANE), 1)
        lane_f = lane_r.astype(F32)
        gsel = jnp.where(lane_r < nblk, gate, NEG)
        pen = jnp.where(lane_r < nblk, NEG, 0.0)
        for _ in range(MOBA_TOPK):
            mx = jnp.max(gsel, axis=1, keepdims=True)
            idx = jnp.min(jnp.where(gsel == mx, lane_f, 1e9), axis=1, keepdims=True)
            pick = (lane_f == idx) & (mx > 0.5 * NEG)
            pen = jnp.where(pick, 0.0, pen)
            gsel = jnp.where(pick, NEG, gsel)

    def scores(p, c):
        s = jnp.dot(q, kbuf[slot, p].astype(BF16), preferred_element_type=F32)
        if kind == "moba":
            s = s + jnp.sum(jnp.where(lane_r == p // ppb, pen, 0.0), axis=1, keepdims=True)
        s_sc[p] = s
        return c

    lax.fori_loop(0, n_pages, scores, 0, unroll=PAGE_UNROLL)
    s_sc[n_pages - 1] = s_sc[n_pages - 1] + bdec_ref[...]
    s_self = jnp.sum(q.astype(F32) * knew_ref[0], axis=1, keepdims=True) + bself_ref[:, 0:1]
    p_self, den = _softmax_pages(s_sc, p_sc, s_self, n_pages)

    def values(p, o):
        return o + _nt_dot(p_sc[p], vbuf[slot, p].astype(BF16))

    on = lax.fori_loop(0, n_pages, values, p_self * vnew_ref[0], unroll=PAGE_UNROLL) / den
    if kind == "diff":
        lam = _lambda_value(lamv_ref[...], lam_init)
        a = on[:N_HEADS] - lam * on[N_HEADS:]
        r16 = lax.broadcasted_iota(jnp.int32, (N_HEADS, kw), 0)
        l16 = lax.broadcasted_iota(jnp.int32, (N_HEADS, kw), 1)
        own = jnp.right_shift(l16, 6) == jnp.right_shift(r16, 2)
        ms = jnp.sum(jnp.where(own, a * a, 0.0), axis=1, keepdims=True) * (1.0 / HEAD_DIM)
        on = a * lax.rsqrt(ms + EPS) * gh_ref[...] * (1.0 - lam_init)
    o_ref[0] = on


def _pos_minor_pages(cache):
    nl, n_pool, page, nkv, hd = cache.shape
    return jnp.transpose(cache, (0, 1, 3, 4, 2)).reshape(nl * n_pool, nkv * hd, page)


def _gqa_decode_attn(page_table, qdec, k_new, v_new, bdec, bself, k_cache, v_cache, layer, kind,
                     lamv=None, gh=None, lam_init=0.0):
    db, n_pages = page_table.shape
    kw = N_KV * HEAD_DIM
    rows = qdec.shape[1] // kw
    n_pool = k_cache.shape[1]
    kc, vc = _pos_minor_pages(k_cache), _pos_minor_pages(v_cache)
    qdec = qdec.reshape(db, rows, kw)
    args = [qdec, k_new.reshape(db, 1, kw), v_new.reshape(db, 1, kw), bdec, bself]
    specs = [pl.BlockSpec((1, rows, kw), lambda b, pt: (b, 0, 0)),
             pl.BlockSpec((1, 1, kw), lambda b, pt: (b, 0, 0)),
             pl.BlockSpec((1, 1, kw), lambda b, pt: (b, 0, 0)),
             pl.BlockSpec((rows, LANE), lambda b, pt: (0, 0)),
             pl.BlockSpec((rows, LANE), lambda b, pt: (0, 0))]
    scratch = [pltpu.VMEM((2, n_pages, kw, PAGE), F32), pltpu.VMEM((2, n_pages, kw, PAGE), F32),
               pltpu.VMEM((n_pages, rows, PAGE), F32), pltpu.VMEM((n_pages, rows, PAGE), BF16)]
    if kind == "diff":
        args += [lamv, gh]
        specs += [pl.BlockSpec((4, LANE), lambda b, pt: (0, 0)), pl.BlockSpec((1, kw), lambda b, pt: (0, 0))]
    else:
        assert n_pages % (MOBA_BLOCK // PAGE) == 0 and n_pages // (MOBA_BLOCK // PAGE) <= LANE
    scratch += [pltpu.SemaphoreType.DMA((2,)), pltpu.SemaphoreType.DMA((2,))]
    args += [kc, vc]
    specs += [pl.BlockSpec(memory_space=pl.ANY), pl.BlockSpec(memory_space=pl.ANY)]
    return pl.pallas_call(
        functools.partial(_gqa_decode_body, kind=kind, n_pages=n_pages, page_off=layer * n_pool, lam_init=lam_init),
        grid_spec=pltpu.PrefetchScalarGridSpec(
            num_scalar_prefetch=1, grid=(db,), in_specs=specs,
            out_specs=pl.BlockSpec((1, N_HEADS, kw), lambda b, pt: (b, 0, 0)),
            scratch_shapes=scratch),
        out_shape=jax.ShapeDtypeStruct((db, N_HEADS, kw), F32),
        compiler_params=_cparams(("arbitrary",), VMEM_LIMIT),
        name=kind + "_decode_attn",
    )(page_table, *args)


def _mla_decode_body(pt_ref, ql_ref, qp_ref, lnew_ref, pnew_ref, lat_hbm, kpe_hbm, o_ref,
                     lbuf, pbuf, s_sc, p_sc, lsem, psem, *, n_pages, page_off):
    slot = _prefetch_pages(pt_ref, n_pages, page_off, [(lat_hbm, lbuf, lsem), (kpe_hbm, pbuf, psem)])
    ql = ql_ref[0]
    qp = qp_ref[0]

    def scores(p, c):
        s_sc[p] = (_nt_dot(ql, lbuf[slot, p].astype(BF16))
                   + jnp.dot(qp, pbuf[slot, p].astype(BF16), preferred_element_type=F32))
        return c

    lax.fori_loop(0, n_pages, scores, 0, unroll=PAGE_UNROLL)
    s_self = (jnp.sum(ql.astype(F32) * lnew_ref[0], axis=1, keepdims=True)
              + jnp.sum(qp.astype(F32) * pnew_ref[0], axis=1, keepdims=True))
    p_self, den = _softmax_pages(s_sc, p_sc, s_self, n_pages)

    def values(p, o):
        return o + jnp.dot(p_sc[p], lbuf[slot, p].astype(BF16), preferred_element_type=F32)

    o_ref[0] = lax.fori_loop(0, n_pages, values, p_self * lnew_ref[0], unroll=PAGE_UNROLL) / den


def _mla_decode_attn(page_table, q_lat, q_pe, lat_new, kpe_new, lat_cache, kpe_cache, layer):
    db, n_pages = page_table.shape
    n_pool = lat_cache.shape[1]
    lc = lat_cache.reshape(lat_cache.shape[0] * n_pool, PAGE, MLA_KV_LORA)
    pc = jnp.transpose(kpe_cache, (0, 1, 3, 2)).reshape(kpe_cache.shape[0] * n_pool, MLA_ROPE, PAGE)
    specs = [pl.BlockSpec((1, N_HEADS, MLA_KV_LORA), lambda b, pt: (b, 0, 0)),
             pl.BlockSpec((1, N_HEADS, MLA_ROPE), lambda b, pt: (b, 0, 0)),
             pl.BlockSpec((1, 1, MLA_KV_LORA), lambda b, pt: (b, 0, 0)),
             pl.BlockSpec((1, 1, MLA_ROPE), lambda b, pt: (b, 0, 0)),
             pl.BlockSpec(memory_space=pl.ANY), pl.BlockSpec(memory_space=pl.ANY)]
    scratch = [pltpu.VMEM((2, n_pages, PAGE, MLA_KV_LORA), F32), pltpu.VMEM((2, n_pages, MLA_ROPE, PAGE), F32),
               pltpu.VMEM((n_pages, N_HEADS, PAGE), F32), pltpu.VMEM((n_pages, N_HEADS, PAGE), BF16),
               pltpu.SemaphoreType.DMA((2,)), pltpu.SemaphoreType.DMA((2,))]
    return pl.pallas_call(
        functools.partial(_mla_decode_body, n_pages=n_pages, page_off=layer * n_pool),
        grid_spec=pltpu.PrefetchScalarGridSpec(
            num_scalar_prefetch=1, grid=(db,), in_specs=specs,
            out_specs=pl.BlockSpec((1, N_HEADS, MLA_KV_LORA), lambda b, pt: (b, 0, 0)),
            scratch_shapes=scratch),
        out_shape=jax.ShapeDtypeStruct((db, N_HEADS, MLA_KV_LORA), F32),
        compiler_params=_cparams(("arbitrary",), VMEM_LIMIT),
        name="mla_decode_attn",
    )(page_table, q_lat, q_pe, lat_new.reshape(db, 1, -1), kpe_new.reshape(db, 1, -1), lc, pc)


def _swa_decode_body(q_ref, k_ref, v_ref, knew_ref, vnew_ref, bdec_ref, bself_ref, sink_ref, o_ref, *, nseq):
    sink = sink_ref[:, 0:1] * LOG2E
    for i in range(nseq):
        q = q_ref[i]
        s = jnp.dot(q, k_ref[i].astype(BF16), preferred_element_type=F32) + bdec_ref[...]
        s_self = jnp.sum(q.astype(F32) * knew_ref[i], axis=1, keepdims=True) + bself_ref[:, 0:1]
        m = jnp.maximum(jnp.maximum(jnp.max(s, axis=1, keepdims=True), s_self), sink)
        p = jnp.exp2(s - m)
        p_self = jnp.exp2(s_self - m)
        den = jnp.sum(p, axis=1, keepdims=True) + p_self + jnp.exp2(sink - m)
        o = _nt_dot(p.astype(BF16), v_ref[i].astype(BF16)) + p_self * vnew_ref[i]
        o_ref[i] = o / den


def _swa_decode_attn(qdec, k_buf, v_buf, k_new, v_new, bdec, bself, sink_rows, nseq=8):
    db, w = k_buf.shape[:2]
    assert w == LANE
    kw = N_KV * HEAD_DIM
    nseq = min(nseq, db)
    qdec = qdec.reshape(db, N_HEADS, kw)
    kt = jnp.transpose(k_buf, (0, 2, 3, 1)).reshape(db, kw, w)
    vt = jnp.transpose(v_buf, (0, 2, 3, 1)).reshape(db, kw, w)
    seq = lambda b: (b, 0, 0)
    full = lambda b: (0, 0)
    return pl.pallas_call(
        functools.partial(_swa_decode_body, nseq=nseq),
        grid=(db // nseq,),
        in_specs=[pl.BlockSpec((nseq, N_HEADS, kw), seq),
                  pl.BlockSpec((nseq, kw, w), seq), pl.BlockSpec((nseq, kw, w), seq),
                  pl.BlockSpec((nseq, 1, kw), seq), pl.BlockSpec((nseq, 1, kw), seq),
                  pl.BlockSpec((N_HEADS, LANE), full), pl.BlockSpec((N_HEADS, LANE), full),
                  pl.BlockSpec((N_HEADS, LANE), full)],
        out_specs=pl.BlockSpec((nseq, N_HEADS, kw), seq),
        out_shape=jax.ShapeDtypeStruct((db, N_HEADS, kw), F32),
        compiler_params=_cparams(("arbitrary",)),
        name="swa_decode_attn",
    )(qdec, kt, vt, k_new.reshape(db, 1, kw), v_new.reshape(db, 1, kw), bdec, bself, sink_rows)


def _rot_cols(w):
    half = w.shape[-1] // 2
    return jnp.concatenate([-w[..., half:], w[..., :half]], axis=-1)


def _pad_lanes(w, width=LANE):
    return jnp.pad(w, [(0, 0)] * (w.ndim - 1) + [(0, width - w.shape[-1])])


def _gqa_weights(w_qkv, scale):
    d = w_qkv.shape[0]
    nq = N_HEADS * HEAD_DIM
    nk = N_KV * HEAD_DIM
    wq = w_qkv[:, :nq].reshape(d, N_HEADS, HEAD_DIM) * (scale * LOG2E)
    wk = w_qkv[:, nq:nq + nk].reshape(d, N_KV, HEAD_DIM)
    wv = w_qkv[:, nq + nk:].reshape(d, N_KV, HEAD_DIM)
    slots = jnp.concatenate([_pad_lanes(wq), _pad_lanes(wk), _pad_lanes(wv)], axis=1)
    return slots.reshape(d, -1).astype(BF16), w_qkv[:, nq:].astype(BF16), wq


def _decode_q_weights(wq, nmap):
    d = wq.shape[0]
    own_group = jnp.eye(N_KV, dtype=wq.dtype).reshape(1, 1, N_KV, 1, N_KV, 1)
    lane_map = (jnp.arange(HEAD_DIM) // (HEAD_DIM // nmap))[None, :] == jnp.arange(nmap)[:, None]
    w = wq.reshape(d, 1, N_KV, GROUP, 1, HEAD_DIM) * lane_map.astype(wq.dtype).reshape(1, nmap, 1, 1, 1, HEAD_DIM)
    return (w * own_group).reshape(d, -1).astype(BF16)


def _decode_o_weights(w_o):
    d = w_o.shape[1]
    own_group = jnp.eye(N_KV, dtype=w_o.dtype).reshape(N_KV, 1, N_KV, 1, 1)
    return (w_o.reshape(N_KV, GROUP, 1, HEAD_DIM, d) * own_group).reshape(-1, d).astype(BF16)


def _rope_tables(pos):
    half = MLA_ROPE // 2
    inv = ROPE_THETA ** (-jnp.arange(half, dtype=F32) / half)
    ang = pos.astype(F32)[:, None] * inv[None, :]
    cos = jnp.concatenate([jnp.cos(ang)] * 2, axis=1)
    sin = jnp.concatenate([jnp.sin(ang)] * 2, axis=1)
    n = pos.shape[0]
    z64, z32, o64 = jnp.zeros((n, 64), F32), jnp.zeros((n, 32), F32), jnp.ones((n, 64), F32)
    cosq = jnp.concatenate([o64, cos, z32], axis=1)
    sinq = jnp.concatenate([z64, sin, z32], axis=1)
    cosk = jnp.concatenate([z64, cos, z32], axis=1)
    return cosq, sinq, cosk


def kernel(x_prompt, x_sample, cache_mla_latent, cache_mla_krope, cache_diff_k, cache_diff_v, cache_moba_k, cache_moba_v, state_swa_k, state_swa_v, state_ffn_conv, page_table, rel_bias, norm_mix_g, norm_ffn_g, norm_final_g, mla_w_dq, mla_g_q, mla_w_uq, mla_w_dkv, mla_g_kv, mla_w_uk, mla_w_uv, mla_w_o, diff_w_qkv, diff_lambda, diff_g_head, diff_w_o, moba_w_qkv, moba_w_o, swa_w_qkv, swa_sinks, swa_w_o, ffn_w_up, ffn_conv_w, ffn_conv_b, ffn_w_down):
    b, s, d = x_prompt.shape
    db = x_sample.shape[0]
    assert x_sample.shape[1] == 1
    depth = norm_mix_g.shape[0]
    past_len = page_table.shape[1] * PAGE
    kw = N_KV * HEAD_DIM
    np_pages = s // PAGE

    tiles = _bias_tiles(rel_bias)
    bias_causal = tiles[T_DIAG:T_PREV + 1].reshape(2, N_KV, GROUP * LANE, LANE)
    bias_swa = tiles[T_SWA_DIAG:T_SWA_PREV + 1].reshape(2, N_KV, GROUP * LANE, LANE)
    bdec = tiles[T_DEC, :, 0, :]
    bself = jnp.broadcast_to(tiles[T_DIAG, :, 0, 0:1], (N_HEADS, LANE))
    bdec_swa = tiles[T_SWA_DEC, :, 0, :]
    bself_swa = jnp.broadcast_to(tiles[T_SWA_DIAG, :, 0, 0:1], (N_HEADS, LANE))

    cos_p = _rope_tables(jnp.arange(s, dtype=jnp.int32))
    cos_s = tuple(jnp.broadcast_to(t, (db, LANE)) for t in _rope_tables(jnp.full((1,), past_len, jnp.int32)))

    hp = x_prompt.reshape(b * s, d)
    hs = x_sample.reshape(db, d)
    outs = {k: [] for k in ("lat_p", "lat_s", "kpe_p", "kpe_s", "dk_p", "dk_s", "dv_p", "dv_s", "mk_p", "mk_s",
                            "mv_p", "mv_s", "sk_p", "sk_s", "sv_p", "sv_s", "conv_p", "conv_s")}

    for i in range(depth):
        kind, l = i % 4, i // 4
        g_mix = norm_mix_g[i]
        if kind == 0:
            w_pe = mla_w_dkv[l][:, MLA_KV_LORA:]
            w1 = jnp.concatenate([mla_w_dq[l], mla_w_dkv[l][:, :MLA_KV_LORA], w_pe, _rot_cols(w_pe),
                                  jnp.zeros((d, LANE - 2 * MLA_ROPE), F32)], axis=1).astype(BF16)
            wuq = mla_w_uq[l] * (MLA_SCALE * LOG2E)
            wuq = jnp.concatenate([wuq, _rot_cols(wuq[..., MLA_NOPE:])], axis=-1).reshape(MLA_Q_LORA, -1).astype(BF16)
            wuk = _pad_lanes(mla_w_uk[l]).reshape(MLA_KV_LORA, -1).astype(BF16)
            wuv = _pad_lanes(mla_w_uv[l]).reshape(MLA_KV_LORA, -1).astype(BF16)
            w_o = mla_w_o[l].astype(BF16)
            (s1,) = _proj(hp, [w1], [F32], g=g_mix, name="mla_down_p")
            qs, ks, vs, lat, kpe = _mla_stage2(s1, mla_g_q[l], mla_g_kv[l], *cos_p, wuq, wuk, wuv)
            o = _mla_prompt_attn(qs, ks, vs, b, s)
            (hp,) = _proj(o.reshape(b * s, -1), [w_o], [F32], res=hp, name="mla_out_p")
            outs["lat_p"].append(lat.reshape(b, np_pages, PAGE, MLA_KV_LORA))
            outs["kpe_p"].append(kpe[:, HEAD_DIM:HEAD_DIM + MLA_ROPE].reshape(b, np_pages, PAGE, MLA_ROPE))
            (s1,) = _proj(hs, [w1], [F32], g=g_mix, name="mla_down_s")
            qs, _, _, lat, kpe = _mla_stage2(s1, mla_g_q[l], mla_g_kv[l], *cos_s, wuq, wuk, wuv)
            kpe = kpe[:, HEAD_DIM:HEAD_DIM + MLA_ROPE]
            wuk_t = jnp.pad(jnp.transpose(mla_w_uk[l], (1, 2, 0)),
                            ((0, 0), (0, LANE - MLA_NOPE), (0, 0))).astype(BF16)
            q_lat = _headwise(qs, wuk_t, BF16, "mla_q_absorb").reshape(db, N_HEADS, MLA_KV_LORA)
            q_pe = qs.reshape(db, N_HEADS, LANE)[:, :, HEAD_DIM:HEAD_DIM + MLA_ROPE]
            o_lat = _mla_decode_attn(page_table, q_lat, q_pe, lat, kpe, cache_mla_latent, cache_mla_krope, l)
            wuv_h = mla_w_uv[l].transpose(1, 0, 2)
            wuv_pair = jnp.concatenate(
                [jnp.concatenate([wuv_h[0::2], jnp.zeros_like(wuv_h[0::2])], axis=-1),
                 jnp.concatenate([jnp.zeros_like(wuv_h[1::2]), wuv_h[1::2]], axis=-1)], axis=1).astype(BF16)
            o = _headwise(o_lat.reshape(db, -1), wuv_pair, BF16, "mla_v_absorb")
            (hs,) = _proj(o, [w_o], [F32], res=hs, name="mla_out_s")
            outs["lat_s"].append(lat.reshape(db, 1, MLA_KV_LORA))
            outs["kpe_s"].append(kpe.reshape(db, 1, MLA_ROPE))
        else:
            w_qkv, w_o, scale, nmap = {
                1: (diff_w_qkv, diff_w_o, DIFF_SCALE, 2),
                2: (moba_w_qkv, moba_w_o, ATTN_SCALE, 1),
                3: (swa_w_qkv, swa_w_o, ATTN_SCALE, 1)}[kind]
            w_slots, w_kv, wq = _gqa_weights(w_qkv[l], scale)
            w_qdec = _decode_q_weights(wq, nmap)
            w_ob = w_o[l].astype(BF16)
            w_odec = _decode_o_weights(w_o[l])
            slots, kv = _proj(hp, [w_slots, w_kv], [BF16, F32], g=g_mix, name="qkv_p%d" % kind)
            qdec, kv_s = _proj(hs, [w_qdec, w_kv], [BF16, F32], g=g_mix, name="qkv_s%d" % kind)
            k_p = kv[:, :kw].reshape(b, s, N_KV, HEAD_DIM)
            v_p = kv[:, kw:].reshape(b, s, N_KV, HEAD_DIM)
            k_s, v_s = kv_s[:, :kw], kv_s[:, kw:]
            if kind == 1:
                lam_init = 0.8 - 0.6 * math.exp(-0.3 * i)
                lamv = _pad_lanes(diff_lambda[l].astype(F32))
                gh = _pad_lanes(diff_g_head[l].astype(F32).reshape(1, -1))
                o = _gqa_prompt_attn(slots, bias_causal, b, s, "diff", lamv, gh, lam_init)
                gh4 = jnp.tile(diff_g_head[l].astype(F32).reshape(1, -1), (1, N_KV))
                o_s = _gqa_decode_attn(page_table, qdec, k_s, v_s, jnp.concatenate([bdec, bdec], axis=0),
                                       jnp.concatenate([bself, bself], axis=0), cache_diff_k, cache_diff_v, l,
                                       "diff", lamv, gh4, lam_init)
                pk, pv, sk, sv = "dk_p", "dv_p", "dk_s", "dv_s"
            elif kind == 2:
                o = _gqa_prompt_attn(slots, bias_causal, b, s, "moba")
                o_s = _gqa_decode_attn(page_table, qdec, k_s, v_s, bdec, bself, cache_moba_k, cache_moba_v, l, "moba")
                pk, pv, sk, sv = "mk_p", "mv_p", "mk_s", "mv_s"
            else:
                o = _swa_prompt_attn(slots, bias_swa, swa_sinks[l], b, s)
                sink_rows = jnp.broadcast_to(swa_sinks[l].astype(F32)[:, None], (N_HEADS, LANE))
                o_s = _swa_decode_attn(qdec, state_swa_k[l], state_swa_v[l], k_s, v_s, bdec_swa, bself_swa, sink_rows)
            (hp,) = _proj(o.reshape(b * s, -1), [w_ob], [F32], res=hp, name="attn_out_p%d" % kind)
            (hs,) = _proj(o_s.reshape(db, -1), [w_odec], [F32], res=hs, name="attn_out_s%d" % kind)
            if kind == 3:
                w_buf = state_swa_k.shape[2]
                outs["sk_p"].append(k_p[:, s - w_buf:])
                outs["sv_p"].append(v_p[:, s - w_buf:])
                outs["sk_s"].append(jnp.concatenate([state_swa_k[l], k_s.reshape(db, 1, N_KV, HEAD_DIM)], axis=1)[:, 1:])
                outs["sv_s"].append(jnp.concatenate([state_swa_v[l], v_s.reshape(db, 1, N_KV, HEAD_DIM)], axis=1)[:, 1:])
            else:
                outs[pk].append(k_p.reshape(b, np_pages, PAGE, N_KV, HEAD_DIM))
                outs[pv].append(v_p.reshape(b, np_pages, PAGE, N_KV, HEAD_DIM))
                outs[sk].append(k_s.reshape(db, 1, N_KV, HEAD_DIM))
                outs[sv].append(v_s.reshape(db, 1, N_KV, HEAD_DIM))

        w_up = ffn_w_up[i].astype(BF16)
        w_down = ffn_w_down[i].astype(BF16)
        hp, conv_p = _ffn_prompt(hp, norm_ffn_g[i], w_up, ffn_conv_w[i], ffn_conv_b[i], w_down, b, s)
        hs, conv_s = _ffn_sample(hs, norm_ffn_g[i], w_up, ffn_conv_w[i], ffn_conv_b[i], w_down, state_ffn_conv[i])
        outs["conv_p"].append(conv_p)
        outs["conv_s"].append(conv_s)

    y_prompt = _final_norm(hp, norm_final_g).reshape(b, s, d)
    y_sample = _final_norm(hs, norm_final_g).reshape(db, 1, d)
    st = lambda k: jnp.stack(outs[k])
    return (y_prompt, y_sample,
            st("lat_p"), st("lat_s"), st("kpe_p"), st("kpe_s"),
            st("dk_p"), st("dk_s"), st("dv_p"), st("dv_s"),
            st("mk_p"), st("mk_s"), st("mv_p"), st("mv_s"),
            st("sk_p"), st("sk_s"), st("sv_p"), st("sv_s"),
            st("conv_p"), st("conv_s"))
```

```python
import functools
import math

import jax
import jax.numpy as jnp
import numpy as np
from jax import lax
from jax.experimental import pallas as pl
from jax.experimental.pallas import tpu as pltpu

F32 = jnp.float32
BF16 = jnp.bfloat16

LANE = 128
HEAD_DIM = 64
N_HEADS = 16
N_KV = 4
GROUP = 4
PAGE = 128
MLA_Q_LORA = 384
MLA_KV_LORA = 256
MLA_NOPE = 64
MLA_ROPE = 32
LOG2E = 1.4426950408889634
MLA_SCALE = (MLA_NOPE + MLA_ROPE) ** -0.5
ATTN_SCALE = HEAD_DIM ** -0.5
DIFF_QK = 32
DIFF_SCALE = DIFF_QK ** -0.5
ROPE_THETA = 10000.0
MOBA_BLOCK = 256
MOBA_TOPK = 3
N_BUCKETS = 32
EPS = 1e-6
NEG = -1e30
ONES_LANE = HEAD_DIM
CHUNK = 512
PAGE_UNROLL = 8
FFN_ROWS = 1024
VMEM_LIMIT = 48 * 1024 * 1024

T_DIAG, T_PREV, T_SWA_DIAG, T_SWA_PREV, T_DEC, T_SWA_DEC = range(6)
_SHIFTED_TILES = (T_DIAG, T_PREV, T_DEC)


def _cparams(sem, vmem=None):
    return pltpu.CompilerParams(dimension_semantics=sem, vmem_limit_bytes=vmem)


def _nt_dot(a, b):
    return lax.dot_general(a, b, (((1,), (1,)), ((), ())), preferred_element_type=F32)


def _rms(x, g):
    return x * lax.rsqrt(jnp.mean(x * x, axis=-1, keepdims=True) + EPS) * g


def _bucket_np(dist):
    n = np.maximum(dist, 0)
    nf = np.maximum(n, 1).astype(np.float32)
    large = 16 + (np.log(nf / np.float32(16)) / np.float32(math.log(8.0)) * np.float32(16)).astype(np.int32)
    large = np.minimum(large, N_BUCKETS - 1)
    return np.where(n < 16, n, large).astype(np.int32)


def _bucket_tiles():
    i = np.arange(LANE)[:, None]
    c = np.arange(LANE)[None, :]
    diag = np.where(i >= c, _bucket_np(i - c), -1)
    prev = _bucket_np(LANE + i - c)
    swa_prev = np.where(c >= i, _bucket_np(LANE + i - c), -1)
    dec = _bucket_np(LANE - c + 0 * i)
    return np.stack([diag, prev, diag, swa_prev, dec, dec]).astype(np.int32)


def _bias_tiles_body(rb_ref, d_ref, o_ref):
    h = pl.program_id(0)
    far = rb_ref[N_BUCKETS - 1, h]
    for t in range(6):
        d = d_ref[t]
        acc = jnp.zeros((LANE, LANE), F32)
        for b in range(N_BUCKETS):
            acc = jnp.where(d == b, rb_ref[b, h], acc)
        if t in _SHIFTED_TILES:
            acc = acc - far
        o_ref[t, 0] = jnp.where(d < 0, NEG, acc * LOG2E)


def _bias_tiles(rel_bias):
    d = jnp.asarray(_bucket_tiles())
    return pl.pallas_call(
        _bias_tiles_body,
        grid=(N_HEADS,),
        in_specs=[pl.BlockSpec(memory_space=pltpu.SMEM),
                  pl.BlockSpec((6, LANE, LANE), lambda h: (0, 0, 0))],
        out_specs=pl.BlockSpec((6, 1, LANE, LANE), lambda h: (0, h, 0, 0)),
        out_shape=jax.ShapeDtypeStruct((6, N_HEADS, LANE, LANE), F32),
        compiler_params=_cparams(("arbitrary",)),
        name="bias_tiles",
    )(rel_bias.astype(F32), d)


def _proj_body(*refs, n_w, norm, residual):
    it = iter(refs)
    x_ref = next(it)
    g_ref = next(it) if norm else None
    r_ref = next(it) if residual else None
    w_refs = [next(it) for _ in range(n_w)]
    o_refs = [next(it) for _ in range(n_w)]
    x = x_ref[...]
    if norm:
        x = _rms(x.astype(F32), g_ref[...])
    xb = x.astype(BF16)
    for k in range(n_w):
        acc = jnp.dot(xb, w_refs[k][...], preferred_element_type=F32)
        if residual and k == 0:
            acc = acc + r_ref[...]
        o_refs[k][...] = acc.astype(o_refs[k].dtype)


def _proj(x, ws, out_dtypes, g=None, res=None, tm=512, name="proj"):
    t, kdim = x.shape
    tm = min(tm, t)
    args = [x]
    specs = [pl.BlockSpec((tm, kdim), lambda i: (i, 0))]
    if g is not None:
        args.append(g.reshape(1, kdim).astype(F32))
        specs.append(pl.BlockSpec((1, kdim), lambda i: (0, 0)))
    if res is not None:
        args.append(res)
        specs.append(pl.BlockSpec((tm, res.shape[1]), lambda i: (i, 0)))
    for w in ws:
        args.append(w)
        specs.append(pl.BlockSpec(w.shape, lambda i: (0, 0)))
    outs = pl.pallas_call(
        functools.partial(_proj_body, n_w=len(ws), norm=g is not None, residual=res is not None),
        grid=(t // tm,),
        in_specs=specs,
        out_specs=[pl.BlockSpec((tm, w.shape[1]), lambda i: (i, 0)) for w in ws],
        out_shape=[jax.ShapeDtypeStruct((t, w.shape[1]), dt) for w, dt in zip(ws, out_dtypes)],
        compiler_params=_cparams(("arbitrary",), VMEM_LIMIT),
        name=name,
    )(*args)
    return outs


def _norm_body(x_ref, g_ref, o_ref):
    o_ref[...] = _rms(x_ref[...], g_ref[...])


def _final_norm(x, g, tm=512):
    t, d = x.shape
    tm = min(tm, t)
    return pl.pallas_call(
        _norm_body,
        grid=(t // tm,),
        in_specs=[pl.BlockSpec((tm, d), lambda i: (i, 0)), pl.BlockSpec((1, d), lambda i: (0, 0))],
        out_specs=pl.BlockSpec((tm, d), lambda i: (i, 0)),
        out_shape=jax.ShapeDtypeStruct((t, d), F32),
        compiler_params=_cparams(("arbitrary",)),
        name="final_norm",
    )(x, g.reshape(1, d))


def _headwise_body(x_ref, w_ref, o_ref):
    o_ref[...] = jnp.dot(x_ref[...].astype(BF16), w_ref[0], preferred_element_type=F32).astype(o_ref.dtype)


def _headwise(x, w, out_dtype, name):
    m = x.shape[0]
    n, kb, nb = w.shape
    return pl.pallas_call(
        _headwise_body,
        grid=(n,),
        in_specs=[pl.BlockSpec((m, kb), lambda i: (0, i)), pl.BlockSpec((1, kb, nb), lambda i: (i, 0, 0))],
        out_specs=pl.BlockSpec((m, nb), lambda i: (0, i)),
        out_shape=jax.ShapeDtypeStruct((m, n * nb), out_dtype),
        compiler_params=_cparams(("arbitrary",)),
        name=name,
    )(x, w)


def _mla_stage2_body(s1_ref, gq_ref, gkv_ref, cq_ref, sq_ref, ck_ref, wuq_ref, wuk_ref, wuv_ref,
                     q_ref, k_ref, v_ref, lat_ref, kpe_ref):
    s1 = s1_ref[...]
    cosq, sinq, cosk = cq_ref[...], sq_ref[...], ck_ref[...]
    cqn = _rms(s1[:, :MLA_Q_LORA], gq_ref[...]).astype(BF16)
    qa = jnp.dot(cqn, wuq_ref[...], preferred_element_type=F32)
    for h in range(N_HEADS):
        blk = qa[:, h * LANE:(h + 1) * LANE]
        q_ref[:, h * LANE:(h + 1) * LANE] = (blk * cosq + pltpu.roll(blk, LANE - MLA_ROPE, axis=1) * sinq).astype(BF16)
    lat = _rms(s1[:, MLA_Q_LORA:MLA_Q_LORA + MLA_KV_LORA], gkv_ref[...])
    lat_ref[...] = lat
    latb = lat.astype(BF16)
    pe = s1[:, MLA_Q_LORA + MLA_KV_LORA:]
    kpe = pltpu.roll(pe, 2 * MLA_ROPE, axis=1) * cosk + pltpu.roll(pe, MLA_ROPE, axis=1) * sinq
    kpe_ref[...] = kpe
    kn = jnp.dot(latb, wuk_ref[...], preferred_element_type=F32)
    vn = jnp.dot(latb, wuv_ref[...], preferred_element_type=F32)
    lane = lax.broadcasted_iota(jnp.int32, kpe.shape, 1)
    ones = jnp.where(lane == ONES_LANE, 1.0, 0.0)
    for h in range(N_HEADS):
        k_ref[:, h * LANE:(h + 1) * LANE] = (kn[:, h * LANE:(h + 1) * LANE] + kpe).astype(BF16)
        v_ref[:, h * LANE:(h + 1) * LANE] = (vn[:, h * LANE:(h + 1) * LANE] + ones).astype(BF16)


def _mla_stage2(s1, g_q, g_kv, cosq, sinq, cosk, wuq, wuk, wuv, tm=256):
    t = s1.shape[0]
    tm = min(tm, t)
    ntab = cosq.shape[0] // tm
    row = lambda i: (i, 0)
    tab = lambda i: (i % ntab, 0)
    full = lambda i: (0, 0)
    hw = N_HEADS * LANE
    return pl.pallas_call(
        _mla_stage2_body,
        grid=(t // tm,),
        in_specs=[pl.BlockSpec((tm, s1.shape[1]), row),
                  pl.BlockSpec((1, MLA_Q_LORA), full), pl.BlockSpec((1, MLA_KV_LORA), full),
                  pl.BlockSpec((tm, LANE), tab), pl.BlockSpec((tm, LANE), tab), pl.BlockSpec((tm, LANE), tab),
                  pl.BlockSpec(wuq.shape, full), pl.BlockSpec(wuk.shape, full), pl.BlockSpec(wuv.shape, full)],
        out_specs=[pl.BlockSpec((tm, hw), row), pl.BlockSpec((tm, hw), row), pl.BlockSpec((tm, hw), row),
                   pl.BlockSpec((tm, MLA_KV_LORA), row), pl.BlockSpec((tm, LANE), row)],
        out_shape=[jax.ShapeDtypeStruct((t, hw), BF16), jax.ShapeDtypeStruct((t, hw), BF16),
                   jax.ShapeDtypeStruct((t, hw), BF16),
                   jax.ShapeDtypeStruct((t, MLA_KV_LORA), F32), jax.ShapeDtypeStruct((t, LANE), F32)],
        compiler_params=_cparams(("arbitrary",), VMEM_LIMIT),
        name="mla_stage2",
    )(s1, g_q.reshape(1, -1), g_kv.reshape(1, -1), cosq, sinq, cosk, wuq, wuk, wuv)


QK_LOOKAHEAD = 3


def _flash_scores(qs, ks, s_wr):
    r = qs[0].shape[0]
    for c, q in enumerate(qs):
        s_wr[c * r:(c + 1) * r, :] = _nt_dot(q, ks[c])


def _flash_stage(qs, ks_next, vs, s_sc, m_sc, acc_sc, biases=None):
    n = len(qs)
    r = qs[0].shape[0]
    pending = {}

    def scores(c):
        if ks_next is not None and c < n:
            pending[c] = _nt_dot(qs[c], ks_next[c])

    m_all, acc_all = m_sc[...], acc_sc[...]
    new_m, new_acc = [], []
    for c in range(QK_LOOKAHEAD):
        scores(c)
    for c in range(n):
        s = s_sc[c * r:(c + 1) * r, :]
        if c in pending:
            s_sc[c * r:(c + 1) * r, :] = pending.pop(c)
        if biases is not None:
            s = s + biases[c]
        cols = [s[:, w * LANE:(w + 1) * LANE] for w in range(s.shape[1] // LANE)]
        mx = cols[0]
        for col in cols[1:]:
            mx = jnp.maximum(mx, col)
        m_prev = m_all[c * r:(c + 1) * r]
        m_new = jnp.maximum(m_prev, jnp.max(mx, axis=1, keepdims=True))
        alpha = jnp.exp2(m_prev - m_new)
        p = jnp.concatenate([jnp.exp2(col - m_new) for col in cols], axis=1).astype(BF16)
        new_acc.append(alpha * acc_all[c * r:(c + 1) * r] + jnp.dot(p, vs[c], preferred_element_type=F32))
        new_m.append(m_new)
        scores(c + QK_LOOKAHEAD)
    m_sc[...] = jnp.concatenate(new_m, axis=0)
    acc_sc[...] = jnp.concatenate(new_acc, axis=0)


def _init_flash(m_sc, acc_sc):
    m_sc[...] = jnp.full(m_sc.shape, NEG, F32)
    acc_sc[...] = jnp.zeros(acc_sc.shape, F32)


def _normalised(acc):
    return acc / acc[:, ONES_LANE:ONES_LANE + 1]


def _mla_prompt_body(q_ref, k_ref, v_ref, o_ref, m_sc, acc_sc, s_sc, *, tq, r, nh):
    qi = pl.program_id(2)
    nrb = tq // r
    qs = [q_ref[0, rb * r:(rb + 1) * r, hh * LANE:(hh + 1) * LANE] for hh in range(nh) for rb in range(nrb)]
    _init_flash(m_sc, acc_sc)

    def chunk(ref, start):
        return [ref[0, pl.ds(start, tq), hh * LANE:(hh + 1) * LANE] for hh in range(nh) for _ in range(nrb)]

    _flash_scores(qs, chunk(k_ref, 0), s_sc)

    def far(j, carry):
        _flash_stage(qs, chunk(k_ref, pl.multiple_of((j + 1) * tq, tq)), chunk(v_ref, pl.multiple_of(j * tq, tq)),
                     s_sc, m_sc, acc_sc)
        return carry

    lax.fori_loop(0, qi, far, 0)
    row = lax.broadcasted_iota(jnp.int32, (r, tq), 0)
    col = lax.broadcasted_iota(jnp.int32, (r, tq), 1)
    causal = [jnp.where(col <= row + rb * r, 0.0, NEG) for rb in range(nrb)] * nh
    _flash_stage(qs, None, chunk(v_ref, pl.multiple_of(qi * tq, tq)), s_sc, m_sc, acc_sc, causal)
    lane = lax.broadcasted_iota(jnp.int32, (r, LANE), 1)
    for pr in range(nh // 2):
        for rb in range(nrb):
            left = _normalised(acc_sc[(2 * pr * nrb + rb) * r:(2 * pr * nrb + rb + 1) * r])
            right = _normalised(acc_sc[((2 * pr + 1) * nrb + rb) * r:((2 * pr + 1) * nrb + rb + 1) * r])
            o_ref[0, rb * r:(rb + 1) * r, pr * LANE:(pr + 1) * LANE] = jnp.where(
                lane < HEAD_DIM, left, pltpu.roll(right, HEAD_DIM, axis=1)).astype(o_ref.dtype)


def _mla_prompt_attn(qs, ks, vs, b, s, tq=CHUNK, r=128, nh=4):
    tq = min(tq, s)
    hw = N_HEADS * LANE
    qs, ks, vs = qs.reshape(b, s, hw), ks.reshape(b, s, hw), vs.reshape(b, s, hw)
    nch = nh * (tq // r)
    return pl.pallas_call(
        functools.partial(_mla_prompt_body, tq=tq, r=r, nh=nh),
        grid=(b, N_HEADS // nh, s // tq),
        in_specs=[pl.BlockSpec((1, tq, nh * LANE), lambda bi, hp, qi: (bi, qi, hp)),
                  pl.BlockSpec((1, s, nh * LANE), lambda bi, hp, qi: (bi, 0, hp)),
                  pl.BlockSpec((1, s, nh * LANE), lambda bi, hp, qi: (bi, 0, hp))],
        out_specs=pl.BlockSpec((1, tq, nh * HEAD_DIM), lambda bi, hp, qi: (bi, qi, hp)),
        out_shape=jax.ShapeDtypeStruct((b, s, N_HEADS * HEAD_DIM), BF16),
        scratch_shapes=[pltpu.VMEM((nch * r, LANE), F32)] * 2 + [pltpu.VMEM((nch * r, tq), F32)],
        compiler_params=_cparams(("arbitrary", "arbitrary", "arbitrary"), VMEM_LIMIT),
        name="mla_prompt_attn",
    )(qs, ks, vs)


def _lambda_value(lamv, lam_init):
    a = jnp.sum(lamv[0:1] * lamv[1:2], axis=1, keepdims=True)
    c = jnp.sum(lamv[2:3] * lamv[3:4], axis=1, keepdims=True)
    return jnp.exp(a) - jnp.exp(c) + lam_init


def _with_ones(v):
    lane = lax.broadcasted_iota(jnp.int32, v.shape, 1)
    return jnp.where(lane == ONES_LANE, 1.0, v.astype(F32)).astype(BF16)


def _gqa_prompt_body(*refs, kind, tq, w, lam_init):
    if kind == "diff":
        q_ref, k_ref, v_ref, bias_ref, lamv_ref, gh_ref, o_ref, q_sc, m_sc, acc_sc, s_sc, v_sc = refs
        nmap = 2
    else:
        q_ref, k_ref, v_ref, bias_ref, o_ref, q_sc, m_sc, acc_sc, s_sc, v_sc, k_sc, kmh_sc, kml_sc = refs
        nmap = 1
    qi = pl.program_id(2)
    r = LANE
    nrb = tq // r
    nch = nmap * GROUP * nrb
    s_len = k_ref.shape[1]
    bpc = w // r
    lane_t = lax.broadcasted_iota(jnp.int32, (r, LANE), 1)

    def q_block(hh, rb):
        return q_ref[0, rb * r:(rb + 1) * r, hh * LANE:(hh + 1) * LANE]

    @pl.when(qi == 0)
    def _():
        for n in range(s_len // MOBA_BLOCK):
            sl = slice(n * MOBA_BLOCK, (n + 1) * MOBA_BLOCK)
            v_sc[sl, :] = _with_ones(v_ref[0, sl, :])
            if kind == "moba":
                kb = k_ref[0, sl, :].astype(F32)
                lane_k = lax.broadcasted_iota(jnp.int32, kb.shape, 1)
                k_sc[sl, :] = jnp.where(lane_k == HEAD_DIM + n, 1.0, kb).astype(BF16)
                if n == 0:
                    kmh_sc[...] = jnp.zeros(kmh_sc.shape, BF16)
                    kml_sc[...] = jnp.zeros(kml_sc.shape, BF16)
                km = jnp.sum(kb, axis=0, keepdims=True) * (1.0 / MOBA_BLOCK)
                hi = km.astype(BF16)
                kmh_sc[HEAD_DIM + n:HEAD_DIM + n + 1, :] = hi
                kml_sc[HEAD_DIM + n:HEAD_DIM + n + 1, :] = (km - hi.astype(F32)).astype(BF16)

    if kind == "diff":
        for mp in range(2):
            keep = (lane_t >= mp * DIFF_QK) & (lane_t < (mp + 1) * DIFF_QK)
            for hh in range(GROUP):
                for rb in range(nrb):
                    r0 = ((mp * GROUP + hh) * nrb + rb) * r
                    q_sc[r0:r0 + r, :] = jnp.where(keep, q_block(hh, rb).astype(F32), 0.0).astype(BF16)
    else:
        nblk = s_len // MOBA_BLOCK
        rows = GROUP * tq
        q0 = jnp.concatenate([q_block(hh, rb) for hh in range(GROUP) for rb in range(nrb)], axis=0)
        gate = _nt_dot(q0, kmh_sc[...]) + _nt_dot(q0, kml_sc[...])
        lane_r = lax.broadcasted_iota(jnp.int32, (rows, LANE), 1)
        lane_f = lane_r.astype(F32)
        assert nrb & (nrb - 1) == 0
        row_id = lax.broadcasted_iota(jnp.int32, (rows, LANE), 0)
        row_blk = jnp.bitwise_and(jnp.right_shift(row_id, r.bit_length() - 1), nrb - 1)
        q_blk = jnp.right_shift(qi * tq + row_blk * r, MOBA_BLOCK.bit_length() - 1)
        elig = (lane_r >= HEAD_DIM) & (lane_r < HEAD_DIM + q_blk)
        gsel = jnp.where(elig, gate, NEG)
        pen = jnp.where((lane_r >= HEAD_DIM) & (lane_r < HEAD_DIM + nblk), NEG, 0.0)
        for _ in range(MOBA_TOPK):
            mx = jnp.max(gsel, axis=1, keepdims=True)
            idx = jnp.min(jnp.where(gsel == mx, lane_f, 1e9), axis=1, keepdims=True)
            pick = (lane_f == idx) & (mx > 0.5 * NEG)
            pen = jnp.where(pick, 0.0, pen)
            gsel = jnp.where(pick, NEG, gsel)
        pen = jnp.where(lane_r == HEAD_DIM + q_blk, 0.0, pen)
        q_sc[...] = (q0.astype(F32) + pen).astype(BF16)

    qs = [q_sc[c * r:(c + 1) * r, :] for c in range(nch)]
    _init_flash(m_sc, acc_sc)
    kk_ref = k_sc if kind == "moba" else k_ref.at[0]

    def chunk_bias(j):
        per_block = {}
        for hh in range(GROUP):
            diag = bias_ref[0, 0, hh * r:(hh + 1) * r, :]
            prev = bias_ref[1, 0, hh * r:(hh + 1) * r, :]
            for rb in range(nrb):
                rel0 = j * bpc - (qi * nrb + rb)
                per_block[hh, rb] = jnp.concatenate(
                    [jnp.where(rel0 + t == 0, diag, jnp.where(rel0 + t == -1, prev, jnp.where(rel0 + t > 0, NEG, 0.0)))
                     for t in range(bpc)], axis=1)
        return [per_block[hh, rb] for _ in range(nmap) for hh in range(GROUP) for rb in range(nrb)]

    def stage(j, has_next, biased):
        ks = [kk_ref[pl.ds(pl.multiple_of((j + 1) * w, w), w), :]] * nch if has_next else None
        v = v_sc[pl.ds(pl.multiple_of(j * w, w), w), :]
        _flash_stage(qs, ks, [v] * nch, s_sc, m_sc, acc_sc, chunk_bias(j) if biased else None)

    qb0 = qi * nrb
    jl = (qb0 + nrb - 1) // bpc
    prev_in_earlier = jnp.logical_and(jl >= 1, jl * bpc >= qb0)
    n_plain = jnp.where(prev_in_earlier, jl - 1, jl)

    _flash_scores(qs, [kk_ref[0:w, :]] * nch, s_sc)

    def plain(j, carry):
        stage(j, True, False)
        return carry

    lax.fori_loop(0, n_plain, plain, 0)

    @pl.when(prev_in_earlier)
    def _():
        stage(jl - 1, True, True)

    stage(jl, False, True)

    def chain_rows(ref, mp, hh, rb):
        c = (mp * GROUP + hh) * nrb + rb
        return ref[c * r:(c + 1) * r]

    for rb in range(nrb):
        heads = []
        for hh in range(GROUP):
            if kind == "diff":
                lam = _lambda_value(lamv_ref[...], lam_init)
                a = _normalised(chain_rows(acc_sc, 0, hh, rb)) - lam * _normalised(chain_rows(acc_sc, 1, hh, rb))
                a = jnp.where(lane_t < HEAD_DIM, a, 0.0)
                ms = jnp.sum(a * a, axis=1, keepdims=True) * (1.0 / HEAD_DIM)
                heads.append(a * lax.rsqrt(ms + EPS) * gh_ref[...] * (1.0 - lam_init))
            else:
                heads.append(_normalised(chain_rows(acc_sc, 0, hh, rb)))
        for pr in range(GROUP // 2):
            lane = lax.broadcasted_iota(jnp.int32, (r, LANE), 1)
            o_ref[0, rb * r:(rb + 1) * r, pr * LANE:(pr + 1) * LANE] = jnp.where(
                lane < HEAD_DIM, heads[2 * pr], pltpu.roll(heads[2 * pr + 1], HEAD_DIM, axis=1)).astype(o_ref.dtype)


def _gqa_prompt_attn(slots, bias, b, s, kind, lamv=None, gh=None, lam_init=0.0, w=CHUNK):
    w = min(w, s)
    tq = min(w, 512 if kind == "moba" else 256)
    assert s % w == 0 and w % tq == 0 and tq % LANE == 0 and s % MOBA_BLOCK == 0 and s // MOBA_BLOCK <= 32
    nslot = slots.shape[1] // LANE
    slots = slots.reshape(b, s, nslot * LANE)
    nmap = 2 if kind == "diff" else 1
    rows = nmap * GROUP * tq
    args = [slots, slots, slots, bias]
    specs = [pl.BlockSpec((1, tq, GROUP * LANE), lambda bi, g, qi: (bi, qi, g)),
             pl.BlockSpec((1, s, LANE), lambda bi, g, qi: (bi, 0, N_HEADS + g)),
             pl.BlockSpec((1, s, LANE), lambda bi, g, qi: (bi, 0, N_HEADS + N_KV + g)),
             pl.BlockSpec((2, 1, GROUP * LANE, LANE), lambda bi, g, qi: (0, g, 0, 0))]
    scratch = [pltpu.VMEM((rows, LANE), BF16), pltpu.VMEM((rows, LANE), F32), pltpu.VMEM((rows, LANE), F32),
               pltpu.VMEM((rows, w), F32), pltpu.VMEM((s, LANE), BF16)]
    if kind == "diff":
        args += [lamv, gh]
        specs += [pl.BlockSpec((4, LANE), lambda bi, g, qi: (0, 0)), pl.BlockSpec((1, LANE), lambda bi, g, qi: (0, 0))]
    else:
        scratch += [pltpu.VMEM((s, LANE), BF16), pltpu.VMEM((LANE, LANE), BF16), pltpu.VMEM((LANE, LANE), BF16)]
    return pl.pallas_call(
        functools.partial(_gqa_prompt_body, kind=kind, tq=tq, w=w, lam_init=lam_init),
        grid=(b, N_KV, s // tq),
        in_specs=specs,
        out_specs=pl.BlockSpec((1, tq, GROUP * HEAD_DIM), lambda bi, g, qi: (bi, qi, g)),
        out_shape=jax.ShapeDtypeStruct((b, s, N_HEADS * HEAD_DIM), BF16),
        scratch_shapes=scratch,
        compiler_params=_cparams(("arbitrary", "arbitrary", "arbitrary"), VMEM_LIMIT),
        name=kind + "_prompt_attn",
    )(*args)


def _swa_prompt_body(sink_ref, q_ref, kp_ref, kc_ref, vp_ref, vc_ref, bias_ref, o_ref, *, nq):
    g = pl.program_id(1)
    qi = pl.program_id(2)
    r = LANE
    sink = jnp.concatenate([jnp.full((r, LANE), sink_ref[g * GROUP + hh] * LOG2E, F32) for hh in range(GROUP)], axis=0)

    def tile(ref_prev, ref_cur, t):
        return ref_prev[0] if t == 0 else ref_cur[0, (t - 1) * r:t * r, :]

    scores = []
    for t in range(nq):
        q = jnp.concatenate([q_ref[0, t * r:(t + 1) * r, hh * LANE:(hh + 1) * LANE] for hh in range(GROUP)], axis=0)
        s_prev = _nt_dot(q, tile(kp_ref, kc_ref, t)) + bias_ref[1, 0]
        if t == 0:
            s_prev = jnp.where(qi == 0, NEG, s_prev)
        scores.append((s_prev, _nt_dot(q, kc_ref[0, t * r:(t + 1) * r, :]) + bias_ref[0, 0]))
    for t, (s_prev, s_cur) in enumerate(scores):
        m = jnp.maximum(jnp.max(jnp.maximum(s_prev, s_cur), axis=1, keepdims=True), sink)
        p_prev = jnp.exp2(s_prev - m)
        p_cur = jnp.exp2(s_cur - m)
        den = jnp.sum(p_prev + p_cur, axis=1, keepdims=True) + jnp.exp2(sink - m)
        o = (jnp.dot(p_prev.astype(BF16), tile(vp_ref, vc_ref, t), preferred_element_type=F32)
             + jnp.dot(p_cur.astype(BF16), vc_ref[0, t * r:(t + 1) * r, :], preferred_element_type=F32)) / den
        lane = lax.broadcasted_iota(jnp.int32, (r, LANE), 1)
        for pr in range(GROUP // 2):
            left, right = o[2 * pr * r:(2 * pr + 1) * r], o[(2 * pr + 1) * r:(2 * pr + 2) * r]
            o_ref[0, t * r:(t + 1) * r, pr * LANE:(pr + 1) * LANE] = jnp.where(
                lane < HEAD_DIM, left, pltpu.roll(right, HEAD_DIM, axis=1)).astype(o_ref.dtype)


def _swa_prompt_attn(slots, bias, sinks, b, s, nq=4):
    nslot = slots.shape[1] // LANE
    slots = slots.reshape(b, s, nslot * LANE)
    tq = nq * LANE
    assert s % tq == 0
    prev = lambda off: (lambda bi, g, qi: (bi, jnp.maximum(nq * qi - 1, 0), off + g))
    cur = lambda off: (lambda bi, g, qi: (bi, qi, off + g))
    return pl.pallas_call(
        functools.partial(_swa_prompt_body, nq=nq),
        grid=(b, N_KV, s // tq),
        in_specs=[pl.BlockSpec(memory_space=pltpu.SMEM),
                  pl.BlockSpec((1, tq, GROUP * LANE), lambda bi, g, qi: (bi, qi, g)),
                  pl.BlockSpec((1, LANE, LANE), prev(N_HEADS)), pl.BlockSpec((1, tq, LANE), cur(N_HEADS)),
                  pl.BlockSpec((1, LANE, LANE), prev(N_HEADS + N_KV)), pl.BlockSpec((1, tq, LANE), cur(N_HEADS + N_KV)),
                  pl.BlockSpec((2, 1, GROUP * LANE, LANE), lambda bi, g, qi: (0, g, 0, 0))],
        out_specs=pl.BlockSpec((1, tq, GROUP * HEAD_DIM), lambda bi, g, qi: (bi, qi, g)),
        out_shape=jax.ShapeDtypeStruct((b, s, N_HEADS * HEAD_DIM), BF16),
        compiler_params=_cparams(("arbitrary", "arbitrary", "arbitrary")),
        name="swa_prompt_attn",
    )(sinks.astype(F32), slots, slots, slots, slots, slots, bias)


def _silu(x):
    return x / (1.0 + jnp.exp(-x))


def _ffn_prompt_body(x_ref, g_ref, wg_ref, wu_ref, cwg_ref, cwu_ref, cbg_ref, cbu_ref, wd_ref,
                     o_ref, sg_ref, su_ref, xn_sc, acc_sc, carry_sc, *, tiles_per_seq):
    i = pl.program_id(0)
    j = pl.program_id(1)
    tm = x_ref.shape[0]

    @pl.when(j == 0)
    def _():
        xn_sc[...] = _rms(x_ref[...], g_ref[...]).astype(BF16)
        acc_sc[...] = jnp.zeros(acc_sc.shape, F32)

    first = (i % tiles_per_seq) == 0
    rbs = min(FFN_ROWS, tm)
    row8 = lax.broadcasted_iota(jnp.int32, (8, wg_ref.shape[1]), 0)

    @pl.when(first)
    def _():
        carry_sc[j] = jnp.zeros(carry_sc.shape[1:], F32)

    def up(rb):
        xb = xn_sc[rb * rbs:(rb + 1) * rbs, :]
        return (jnp.dot(xb, wg_ref[...], preferred_element_type=F32),
                jnp.dot(xb, wu_ref[...], preferred_element_type=F32))

    def conv(h, tail, cw_ref, cb_ref):
        p6, p7 = tail[6:7], tail[7:8]
        h1, h2 = pltpu.roll(h, 1, axis=0), pltpu.roll(h, 2, axis=0)
        h1 = jnp.concatenate([jnp.where(row8 == 0, p7, h1[:8]), h1[8:]], axis=0)
        h2 = jnp.concatenate([jnp.where(row8 == 0, p6, jnp.where(row8 == 1, p7, h2[:8])), h2[8:]], axis=0)
        return cb_ref[...] + cw_ref[0:1] * h2 + cw_ref[1:2] * h1 + cw_ref[2:3] * h

    tail_g, tail_u = carry_sc[j, 0], carry_sc[j, 1]
    pending = up(0)
    for rb in range(tm // rbs):
        hg, hu = pending
        if (rb + 1) * rbs < tm:
            pending = up(rb + 1)
        act = _silu(conv(hg, tail_g, cwg_ref, cbg_ref)) * conv(hu, tail_u, cwu_ref, cbu_ref)
        acc_sc[rb * rbs:(rb + 1) * rbs, :] += jnp.dot(act.astype(BF16), wd_ref[...], preferred_element_type=F32)
        tail_g, tail_u = hg[rbs - 8:], hu[rbs - 8:]
    carry_sc[j, 0] = tail_g
    carry_sc[j, 1] = tail_u
    sg_ref[0] = tail_g
    su_ref[0] = tail_u

    @pl.when(j == pl.num_programs(1) - 1)
    def _():
        o_ref[...] = x_ref[...] + acc_sc[...]


def _ffn_prompt(x, g, w_up, conv_w, conv_b, w_down, b, s, tm=1024, fc=256):
    t, d = x.shape
    dff = w_down.shape[0]
    tm = min(tm, s)
    nf = dff // fc
    row = lambda i, j: (i, 0)
    gcol = lambda i, j: (0, j)
    ucol = lambda i, j: (0, nf + j)
    tps = s // tm
    out, sg, su = pl.pallas_call(
        functools.partial(_ffn_prompt_body, tiles_per_seq=tps),
        grid=(t // tm, nf),
        in_specs=[pl.BlockSpec((tm, d), row), pl.BlockSpec((1, d), lambda i, j: (0, 0)),
                  pl.BlockSpec((d, fc), gcol), pl.BlockSpec((d, fc), ucol),
                  pl.BlockSpec((3, fc), gcol), pl.BlockSpec((3, fc), ucol),
                  pl.BlockSpec((1, fc), gcol), pl.BlockSpec((1, fc), ucol),
                  pl.BlockSpec((fc, d), lambda i, j: (j, 0))],
        out_specs=[pl.BlockSpec((tm, d), row),
                   pl.BlockSpec((1, 8, fc), lambda i, j: (i, 0, j)),
                   pl.BlockSpec((1, 8, fc), lambda i, j: (i, 0, j))],
        out_shape=[jax.ShapeDtypeStruct((t, d), F32), jax.ShapeDtypeStruct((t // tm, 8, dff), F32),
                   jax.ShapeDtypeStruct((t // tm, 8, dff), F32)],
        scratch_shapes=[pltpu.VMEM((tm, d), BF16), pltpu.VMEM((tm, d), F32), pltpu.VMEM((nf, 2, 8, fc), F32)],
        compiler_params=_cparams(("arbitrary", "arbitrary"), VMEM_LIMIT),
        name="ffn_prompt",
    )(x, g.reshape(1, d), w_up, w_up, conv_w, conv_w, conv_b.reshape(1, -1), conv_b.reshape(1, -1), w_down)
    last = slice(tps - 1, None, tps)
    state = jnp.concatenate([sg[last, 6:8], su[last, 6:8]], axis=-1)
    return out, state


def _ffn_sample_body(x_ref, g_ref, wg_ref, wu_ref, cwg_ref, cwu_ref, cbg_ref, cbu_ref, wd_ref,
                     s0g_ref, s1g_ref, s0u_ref, s1u_ref, o_ref, hg_ref, hu_ref, xn_sc, acc_sc):
    j = pl.program_id(0)

    @pl.when(j == 0)
    def _():
        xn_sc[...] = _rms(x_ref[...], g_ref[...]).astype(BF16)
        acc_sc[...] = jnp.zeros(acc_sc.shape, F32)

    xb = xn_sc[...]
    hg = jnp.dot(xb, wg_ref[...], preferred_element_type=F32)
    hu = jnp.dot(xb, wu_ref[...], preferred_element_type=F32)
    hg_ref[...] = hg
    hu_ref[...] = hu
    cg = cbg_ref[...] + cwg_ref[0:1] * s0g_ref[...] + cwg_ref[1:2] * s1g_ref[...] + cwg_ref[2:3] * hg
    cu = cbu_ref[...] + cwu_ref[0:1] * s0u_ref[...] + cwu_ref[1:2] * s1u_ref[...] + cwu_ref[2:3] * hu
    acc_sc[...] += jnp.dot((_silu(cg) * cu).astype(BF16), wd_ref[...], preferred_element_type=F32)

    @pl.when(j == pl.num_programs(0) - 1)
    def _():
        o_ref[...] = x_ref[...] + acc_sc[...]


def _ffn_sample(x, g, w_up, conv_w, conv_b, w_down, state, fc=256):
    t, d = x.shape
    dff = w_down.shape[0]
    nf = dff // fc
    s0, s1 = state[:, 0], state[:, 1]
    full = lambda j: (0, 0)
    gcol = lambda j: (0, j)
    ucol = lambda j: (0, nf + j)
    out, hg, hu = pl.pallas_call(
        _ffn_sample_body,
        grid=(nf,),
        in_specs=[pl.BlockSpec((t, d), full), pl.BlockSpec((1, d), full),
                  pl.BlockSpec((d, fc), gcol), pl.BlockSpec((d, fc), ucol),
                  pl.BlockSpec((3, fc), gcol), pl.BlockSpec((3, fc), ucol),
                  pl.BlockSpec((1, fc), gcol), pl.BlockSpec((1, fc), ucol),
                  pl.BlockSpec((fc, d), lambda j: (j, 0)),
                  pl.BlockSpec((t, fc), gcol), pl.BlockSpec((t, fc), gcol),
                  pl.BlockSpec((t, fc), ucol), pl.BlockSpec((t, fc), ucol)],
        out_specs=[pl.BlockSpec((t, d), full), pl.BlockSpec((t, fc), gcol), pl.BlockSpec((t, fc), gcol)],
        out_shape=[jax.ShapeDtypeStruct((t, d), F32), jax.ShapeDtypeStruct((t, dff), F32),
                   jax.ShapeDtypeStruct((t, dff), F32)],
        scratch_shapes=[pltpu.VMEM((t, d), BF16), pltpu.VMEM((t, d), F32)],
        compiler_params=_cparams(("arbitrary",), VMEM_LIMIT),
        name="ffn_sample",
    )(x, g.reshape(1, d), w_up, w_up, conv_w, conv_w, conv_b.reshape(1, -1), conv_b.reshape(1, -1), w_down,
      s0, s1, s0, s1)
    h = jnp.concatenate([hg, hu], axis=-1)
    return out, jnp.stack([s1, h], axis=1)


def _page_copies(pt_ref, seq, slot, n_pages, page_off, pairs):
    def make(p, cache, buf, sem):
        pg = pt_ref[seq, p] + page_off
        return pltpu.make_async_copy(cache.at[pg], buf.at[slot, p], sem.at[slot])

    def start(p, c):
        for cache, buf, sem in pairs:
            make(p, cache, buf, sem).start()
        return c

    def wait(p, c):
        for cache, buf, sem in pairs:
            make(p, cache, buf, sem).wait()
        return c

    return (lambda: lax.fori_loop(0, n_pages, start, 0, unroll=PAGE_UNROLL)), (
        lambda: lax.fori_loop(0, n_pages, wait, 0, unroll=PAGE_UNROLL))


def _prefetch_pages(pt_ref, n_pages, page_off, pairs):
    b = pl.program_id(0)
    nb = pl.num_programs(0)

    @pl.when(b == 0)
    def _():
        _page_copies(pt_ref, 0, 0, n_pages, page_off, pairs)[0]()

    @pl.when(b + 1 < nb)
    def _():
        _page_copies(pt_ref, b + 1, (b + 1) % 2, n_pages, page_off, pairs)[0]()

    slot = b % 2
    _page_copies(pt_ref, b, slot, n_pages, page_off, pairs)[1]()
    return slot


def _softmax_pages(s_sc, p_sc, s_self, n_pages):
    s = s_sc[...]
    m = jnp.maximum(jnp.max(jnp.max(s, axis=0), axis=1, keepdims=True), s_self)
    p = jnp.exp2(s - m[None])
    p_self = jnp.exp2(s_self - m)
    p_sc[...] = p.astype(BF16)
    return p_self, jnp.sum(jnp.sum(p, axis=0), axis=1, keepdims=True) + p_self


def _gqa_decode_body(pt_ref, q_ref, knew_ref, vnew_ref, bdec_ref, bself_ref, *rest,
                     kind, n_pages, page_off, lam_init):
    if kind == "diff":
        lamv_ref, gh_ref, kc_hbm, vc_hbm, o_ref, kbuf, vbuf, s_sc, p_sc, ksem, vsem = rest
    else:
        kc_hbm, vc_hbm, o_ref, kbuf, vbuf, s_sc, p_sc, ksem, vsem = rest
    slot = _prefetch_pages(pt_ref, n_pages, page_off, [(kc_hbm, kbuf, ksem), (vc_hbm, vbuf, vsem)])
    q = q_ref[0]
    rows, kw = q.shape

    if kind == "moba":
        ppb = MOBA_BLOCK // PAGE
        nblk = n_pages // ppb
        lane_k = lax.broadcasted_iota(jnp.int32, (kw, LANE), 1)
        kmt = jnp.zeros((kw, LANE), F32)
        for n in range(nblk):
            blk = kbuf[slot, ppb * n]
            for e in range(1, ppb):
                blk = blk + kbuf[slot, ppb * n + e]
            kmt = jnp.where(lane_k == n, jnp.sum(blk, axis=1, keepdims=True) * (1.0 / MOBA_BLOCK), kmt)
        hi = kmt.astype(BF16)
        lo = (kmt - hi.astype(F32)).astype(BF16)
        gate = jnp.dot(q, hi, preferred_element_type=F32) + jnp.dot(q, lo, preferred_element_type=F32)
        lane_r = lax.broadcasted_iota(jnp.int32, (rows, LANE), 1)
        lane_f = lane_r.astype(F32)
        gsel = jnp.where(lane_r < nblk, gate, NEG)
        pen = jnp.where(lane_r < nblk, NEG, 0.0)
        for _ in range(MOBA_TOPK):
            mx = jnp.max(gsel, axis=1, keepdims=True)
            idx = jnp.min(jnp.where(gsel == mx, lane_f, 1e9), axis=1, keepdims=True)
            pick = (lane_f == idx) & (mx > 0.5 * NEG)
            pen = jnp.where(pick, 0.0, pen)
            gsel = jnp.where(pick, NEG, gsel)

    def scores(p, c):
        s = jnp.dot(q, kbuf[slot, p].astype(BF16), preferred_element_type=F32)
        if kind == "moba":
            s = s + jnp.sum(jnp.where(lane_r == p // ppb, pen, 0.0), axis=1, keepdims=True)
        s_sc[p] = s
        return c

    lax.fori_loop(0, n_pages, scores, 0, unroll=PAGE_UNROLL)
    s_sc[n_pages - 1] = s_sc[n_pages - 1] + bdec_ref[...]
    s_self = jnp.sum(q.astype(F32) * knew_ref[0], axis=1, keepdims=True) + bself_ref[:, 0:1]
    p_self, den = _softmax_pages(s_sc, p_sc, s_self, n_pages)

    def values(p, o):
        return o + _nt_dot(p_sc[p], vbuf[slot, p].astype(BF16))

    on = lax.fori_loop(0, n_pages, values, p_self * vnew_ref[0], unroll=PAGE_UNROLL) / den
    if kind == "diff":
        lam = _lambda_value(lamv_ref[...], lam_init)
        a = on[:N_HEADS] - lam * on[N_HEADS:]
        r16 = lax.broadcasted_iota(jnp.int32, (N_HEADS, kw), 0)
        l16 = lax.broadcasted_iota(jnp.int32, (N_HEADS, kw), 1)
        own = jnp.right_shift(l16, 6) == jnp.right_shift(r16, 2)
        ms = jnp.sum(jnp.where(own, a * a, 0.0), axis=1, keepdims=True) * (1.0 / HEAD_DIM)
        on = a * lax.rsqrt(ms + EPS) * gh_ref[...] * (1.0 - lam_init)
    o_ref[0] = on


def _pos_minor_pages(cache):
    nl, n_pool, page, nkv, hd = cache.shape
    return jnp.transpose(cache, (0, 1, 3, 4, 2)).reshape(nl * n_pool, nkv * hd, page)


def _gqa_decode_attn(page_table, qdec, k_new, v_new, bdec, bself, k_cache, v_cache, layer, kind,
                     lamv=None, gh=None, lam_init=0.0):
    db, n_pages = page_table.shape
    kw = N_KV * HEAD_DIM
    rows = qdec.shape[1] // kw
    n_pool = k_cache.shape[1]
    kc, vc = _pos_minor_pages(k_cache), _pos_minor_pages(v_cache)
    qdec = qdec.reshape(db, rows, kw)
    args = [qdec, k_new.reshape(db, 1, kw), v_new.reshape(db, 1, kw), bdec, bself]
    specs = [pl.BlockSpec((1, rows, kw), lambda b, pt: (b, 0, 0)),
             pl.BlockSpec((1, 1, kw), lambda b, pt: (b, 0, 0)),
             pl.BlockSpec((1, 1, kw), lambda b, pt: (b, 0, 0)),
             pl.BlockSpec((rows, LANE), lambda b, pt: (0, 0)),
             pl.BlockSpec((rows, LANE), lambda b, pt: (0, 0))]
    scratch = [pltpu.VMEM((2, n_pages, kw, PAGE), F32), pltpu.VMEM((2, n_pages, kw, PAGE), F32),
               pltpu.VMEM((n_pages, rows, PAGE), F32), pltpu.VMEM((n_pages, rows, PAGE), BF16)]
    if kind == "diff":
        args += [lamv, gh]
        specs += [pl.BlockSpec((4, LANE), lambda b, pt: (0, 0)), pl.BlockSpec((1, kw), lambda b, pt: (0, 0))]
    else:
        assert n_pages % (MOBA_BLOCK // PAGE) == 0 and n_pages // (MOBA_BLOCK // PAGE) <= LANE
    scratch += [pltpu.SemaphoreType.DMA((2,)), pltpu.SemaphoreType.DMA((2,))]
    args += [kc, vc]
    specs += [pl.BlockSpec(memory_space=pl.ANY), pl.BlockSpec(memory_space=pl.ANY)]
    return pl.pallas_call(
        functools.partial(_gqa_decode_body, kind=kind, n_pages=n_pages, page_off=layer * n_pool, lam_init=lam_init),
        grid_spec=pltpu.PrefetchScalarGridSpec(
            num_scalar_prefetch=1, grid=(db,), in_specs=specs,
            out_specs=pl.BlockSpec((1, N_HEADS, kw), lambda b, pt: (b, 0, 0)),
            scratch_shapes=scratch),
        out_shape=jax.ShapeDtypeStruct((db, N_HEADS, kw), F32),
        compiler_params=_cparams(("arbitrary",), VMEM_LIMIT),
        name=kind + "_decode_attn",
    )(page_table, *args)


def _mla_decode_body(pt_ref, ql_ref, qp_ref, lnew_ref, pnew_ref, lat_hbm, kpe_hbm, o_ref,
                     lbuf, pbuf, s_sc, p_sc, lsem, psem, *, n_pages, page_off):
    slot = _prefetch_pages(pt_ref, n_pages, page_off, [(lat_hbm, lbuf, lsem), (kpe_hbm, pbuf, psem)])
    ql = ql_ref[0]
    qp = qp_ref[0]

    def scores(p, c):
        s_sc[p] = (_nt_dot(ql, lbuf[slot, p].astype(BF16))
                   + jnp.dot(qp, pbuf[slot, p].astype(BF16), preferred_element_type=F32))
        return c

    lax.fori_loop(0, n_pages, scores, 0, unroll=PAGE_UNROLL)
    s_self = (jnp.sum(ql.astype(F32) * lnew_ref[0], axis=1, keepdims=True)
              + jnp.sum(qp.astype(F32) * pnew_ref[0], axis=1, keepdims=True))
    p_self, den = _softmax_pages(s_sc, p_sc, s_self, n_pages)

    def values(p, o):
        return o + jnp.dot(p_sc[p], lbuf[slot, p].astype(BF16), preferred_element_type=F32)

    o_ref[0] = lax.fori_loop(0, n_pages, values, p_self * lnew_ref[0], unroll=PAGE_UNROLL) / den


def _mla_decode_attn(page_table, q_lat, q_pe, lat_new, kpe_new, lat_cache, kpe_cache, layer):
    db, n_pages = page_table.shape
    n_pool = lat_cache.shape[1]
    lc = lat_cache.reshape(lat_cache.shape[0] * n_pool, PAGE, MLA_KV_LORA)
    pc = jnp.transpose(kpe_cache, (0, 1, 3, 2)).reshape(kpe_cache.shape[0] * n_pool, MLA_ROPE, PAGE)
    specs = [pl.BlockSpec((1, N_HEADS, MLA_KV_LORA), lambda b, pt: (b, 0, 0)),
             pl.BlockSpec((1, N_HEADS, MLA_ROPE), lambda b, pt: (b, 0, 0)),
             pl.BlockSpec((1, 1, MLA_KV_LORA), lambda b, pt: (b, 0, 0)),
             pl.BlockSpec((1, 1, MLA_ROPE), lambda b, pt: (b, 0, 0)),
             pl.BlockSpec(memory_space=pl.ANY), pl.BlockSpec(memory_space=pl.ANY)]
    scratch = [pltpu.VMEM((2, n_pages, PAGE, MLA_KV_LORA), F32), pltpu.VMEM((2, n_pages, MLA_ROPE, PAGE), F32),
               pltpu.VMEM((n_pages, N_HEADS, PAGE), F32), pltpu.VMEM((n_pages, N_HEADS, PAGE), BF16),
               pltpu.SemaphoreType.DMA((2,)), pltpu.SemaphoreType.DMA((2,))]
    return pl.pallas_call(
        functools.partial(_mla_decode_body, n_pages=n_pages, page_off=layer * n_pool),
        grid_spec=pltpu.PrefetchScalarGridSpec(
            num_scalar_prefetch=1, grid=(db,), in_specs=specs,
            out_specs=pl.BlockSpec((1, N_HEADS, MLA_KV_LORA), lambda b, pt: (b, 0, 0)),
            scratch_shapes=scratch),
        out_shape=jax.ShapeDtypeStruct((db, N_HEADS, MLA_KV_LORA), F32),
        compiler_params=_cparams(("arbitrary",), VMEM_LIMIT),
        name="mla_decode_attn",
    )(page_table, q_lat, q_pe, lat_new.reshape(db, 1, -1), kpe_new.reshape(db, 1, -1), lc, pc)


def _swa_decode_body(q_ref, k_ref, v_ref, knew_ref, vnew_ref, bdec_ref, bself_ref, sink_ref, o_ref, *, nseq):
    sink = sink_ref[:, 0:1] * LOG2E
    for i in range(nseq):
        q = q_ref[i]
        s = jnp.dot(q, k_ref[i].astype(BF16), preferred_element_type=F32) + bdec_ref[...]
        s_self = jnp.sum(q.astype(F32) * knew_ref[i], axis=1, keepdims=True) + bself_ref[:, 0:1]
        m = jnp.maximum(jnp.maximum(jnp.max(s, axis=1, keepdims=True), s_self), sink)
        p = jnp.exp2(s - m)
        p_self = jnp.exp2(s_self - m)
        den = jnp.sum(p, axis=1, keepdims=True) + p_self + jnp.exp2(sink - m)
        o = _nt_dot(p.astype(BF16), v_ref[i].astype(BF16)) + p_self * vnew_ref[i]
        o_ref[i] = o / den


def _swa_decode_attn(qdec, k_buf, v_buf, k_new, v_new, bdec, bself, sink_rows, nseq=8):
    db, w = k_buf.shape[:2]
    assert w == LANE
    kw = N_KV * HEAD_DIM
    nseq = min(nseq, db)
    qdec = qdec.reshape(db, N_HEADS, kw)
    kt = jnp.transpose(k_buf, (0, 2, 3, 1)).reshape(db, kw, w)
    vt = jnp.transpose(v_buf, (0, 2, 3, 1)).reshape(db, kw, w)
    seq = lambda b: (b, 0, 0)
    full = lambda b: (0, 0)
    return pl.pallas_call(
        functools.partial(_swa_decode_body, nseq=nseq),
        grid=(db // nseq,),
        in_specs=[pl.BlockSpec((nseq, N_HEADS, kw), seq),
                  pl.BlockSpec((nseq, kw, w), seq), pl.BlockSpec((nseq, kw, w), seq),
                  pl.BlockSpec((nseq, 1, kw), seq), pl.BlockSpec((nseq, 1, kw), seq),
                  pl.BlockSpec((N_HEADS, LANE), full), pl.BlockSpec((N_HEADS, LANE), full),
                  pl.BlockSpec((N_HEADS, LANE), full)],
        out_specs=pl.BlockSpec((nseq, N_HEADS, kw), seq),
        out_shape=jax.ShapeDtypeStruct((db, N_HEADS, kw), F32),
        compiler_params=_cparams(("arbitrary",)),
        name="swa_decode_attn",
    )(qdec, kt, vt, k_new.reshape(db, 1, kw), v_new.reshape(db, 1, kw), bdec, bself, sink_rows)


def _rot_cols(w):
    half = w.shape[-1] // 2
    return jnp.concatenate([-w[..., half:], w[..., :half]], axis=-1)


def _pad_lanes(w, width=LANE):
    return jnp.pad(w, [(0, 0)] * (w.ndim - 1) + [(0, width - w.shape[-1])])


def _gqa_weights(w_qkv, scale):
    d = w_qkv.shape[0]
    nq = N_HEADS * HEAD_DIM
    nk = N_KV * HEAD_DIM
    wq = w_qkv[:, :nq].reshape(d, N_HEADS, HEAD_DIM) * (scale * LOG2E)
    wk = w_qkv[:, nq:nq + nk].reshape(d, N_KV, HEAD_DIM)
    wv = w_qkv[:, nq + nk:].reshape(d, N_KV, HEAD_DIM)
    slots = jnp.concatenate([_pad_lanes(wq), _pad_lanes(wk), _pad_lanes(wv)], axis=1)
    return slots.reshape(d, -1).astype(BF16), w_qkv[:, nq:].astype(BF16), wq


def _decode_q_weights(wq, nmap):
    d = wq.shape[0]
    own_group = jnp.eye(N_KV, dtype=wq.dtype).reshape(1, 1, N_KV, 1, N_KV, 1)
    lane_map = (jnp.arange(HEAD_DIM) // (HEAD_DIM // nmap))[None, :] == jnp.arange(nmap)[:, None]
    w = wq.reshape(d, 1, N_KV, GROUP, 1, HEAD_DIM) * lane_map.astype(wq.dtype).reshape(1, nmap, 1, 1, 1, HEAD_DIM)
    return (w * own_group).reshape(d, -1).astype(BF16)


def _decode_o_weights(w_o):
    d = w_o.shape[1]
    own_group = jnp.eye(N_KV, dtype=w_o.dtype).reshape(N_KV, 1, N_KV, 1, 1)
    return (w_o.reshape(N_KV, GROUP, 1, HEAD_DIM, d) * own_group).reshape(-1, d).astype(BF16)


def _rope_tables(pos):
    half = MLA_ROPE // 2
    inv = ROPE_THETA ** (-jnp.arange(half, dtype=F32) / half)
    ang = pos.astype(F32)[:, None] * inv[None, :]
    cos = jnp.concatenate([jnp.cos(ang)] * 2, axis=1)
    sin = jnp.concatenate([jnp.sin(ang)] * 2, axis=1)
    n = pos.shape[0]
    z64, z32, o64 = jnp.zeros((n, 64), F32), jnp.zeros((n, 32), F32), jnp.ones((n, 64), F32)
    cosq = jnp.concatenate([o64, cos, z32], axis=1)
    sinq = jnp.concatenate([z64, sin, z32], axis=1)
    cosk = jnp.concatenate([z64, cos, z32], axis=1)
    return cosq, sinq, cosk


def kernel(x_prompt, x_sample, cache_mla_latent, cache_mla_krope, cache_diff_k, cache_diff_v, cache_moba_k, cache_moba_v, state_swa_k, state_swa_v, state_ffn_conv, page_table, rel_bias, norm_mix_g, norm_ffn_g, norm_final_g, mla_w_dq, mla_g_q, mla_w_uq, mla_w_dkv, mla_g_kv, mla_w_uk, mla_w_uv, mla_w_o, diff_w_qkv, diff_lambda, diff_g_head, diff_w_o, moba_w_qkv, moba_w_o, swa_w_qkv, swa_sinks, swa_w_o, ffn_w_up, ffn_conv_w, ffn_conv_b, ffn_w_down):
    b, s, d = x_prompt.shape
    db = x_sample.shape[0]
    assert x_sample.shape[1] == 1
    depth = norm_mix_g.shape[0]
    past_len = page_table.shape[1] * PAGE
    kw = N_KV * HEAD_DIM
    np_pages = s // PAGE

    tiles = _bias_tiles(rel_bias)
    bias_causal = tiles[T_DIAG:T_PREV + 1].reshape(2, N_KV, GROUP * LANE, LANE)
    bias_swa = tiles[T_SWA_DIAG:T_SWA_PREV + 1].reshape(2, N_KV, GROUP * LANE, LANE)
    bdec = tiles[T_DEC, :, 0, :]
    bself = jnp.broadcast_to(tiles[T_DIAG, :, 0, 0:1], (N_HEADS, LANE))
    bdec_swa = tiles[T_SWA_DEC, :, 0, :]
    bself_swa = jnp.broadcast_to(tiles[T_SWA_DIAG, :, 0, 0:1], (N_HEADS, LANE))

    cos_p = _rope_tables(jnp.arange(s, dtype=jnp.int32))
    cos_s = tuple(jnp.broadcast_to(t, (db, LANE)) for t in _rope_tables(jnp.full((1,), past_len, jnp.int32)))

    hp = x_prompt.reshape(b * s, d)
    hs = x_sample.reshape(db, d)
    outs = {k: [] for k in ("lat_p", "lat_s", "kpe_p", "kpe_s", "dk_p", "dk_s", "dv_p", "dv_s", "mk_p", "mk_s",
                            "mv_p", "mv_s", "sk_p", "sk_s", "sv_p", "sv_s", "conv_p", "conv_s")}

    for i in range(depth):
        kind, l = i % 4, i // 4
        g_mix = norm_mix_g[i]
        if kind == 0:
            w_pe = mla_w_dkv[l][:, MLA_KV_LORA:]
            w1 = jnp.concatenate([mla_w_dq[l], mla_w_dkv[l][:, :MLA_KV_LORA], w_pe, _rot_cols(w_pe),
                                  jnp.zeros((d, LANE - 2 * MLA_ROPE), F32)], axis=1).astype(BF16)
            wuq = mla_w_uq[l] * (MLA_SCALE * LOG2E)
            wuq = jnp.concatenate([wuq, _rot_cols(wuq[..., MLA_NOPE:])], axis=-1).reshape(MLA_Q_LORA, -1).astype(BF16)
            wuk = _pad_lanes(mla_w_uk[l]).reshape(MLA_KV_LORA, -1).astype(BF16)
            wuv = _pad_lanes(mla_w_uv[l]).reshape(MLA_KV_LORA, -1).astype(BF16)
            w_o = mla_w_o[l].astype(BF16)
            (s1,) = _proj(hp, [w1], [F32], g=g_mix, name="mla_down_p")
            qs, ks, vs, lat, kpe = _mla_stage2(s1, mla_g_q[l], mla_g_kv[l], *cos_p, wuq, wuk, wuv)
            o = _mla_prompt_attn(qs, ks, vs, b, s)
            (hp,) = _proj(o.reshape(b * s, -1), [w_o], [F32], res=hp, name="mla_out_p")
            outs["lat_p"].append(lat.reshape(b, np_pages, PAGE, MLA_KV_LORA))
            outs["kpe_p"].append(kpe[:, HEAD_DIM:HEAD_DIM + MLA_ROPE].reshape(b, np_pages, PAGE, MLA_ROPE))
            (s1,) = _proj(hs, [w1], [F32], g=g_mix, name="mla_down_s")
            qs, _, _, lat, kpe = _mla_stage2(s1, mla_g_q[l], mla_g_kv[l], *cos_s, wuq, wuk, wuv)
            kpe = kpe[:, HEAD_DIM:HEAD_DIM + MLA_ROPE]
            wuk_t = jnp.pad(jnp.transpose(mla_w_uk[l], (1, 2, 0)),
                            ((0, 0), (0, LANE - MLA_NOPE), (0, 0))).astype(BF16)
            q_lat = _headwise(qs, wuk_t, BF16, "mla_q_absorb").reshape(db, N_HEADS, MLA_KV_LORA)
            q_pe = qs.reshape(db, N_HEADS, LANE)[:, :, HEAD_DIM:HEAD_DIM + MLA_ROPE]
            o_lat = _mla_decode_attn(page_table, q_lat, q_pe, lat, kpe, cache_mla_latent, cache_mla_krope, l)
            wuv_h = mla_w_uv[l].transpose(1, 0, 2)
            wuv_pair = jnp.concatenate(
                [jnp.concatenate([wuv_h[0::2], jnp.zeros_like(wuv_h[0::2])], axis=-1),
                 jnp.concatenate([jnp.zeros_like(wuv_h[1::2]), wuv_h[1::2]], axis=-1)], axis=1).astype(BF16)
            o = _headwise(o_lat.reshape(db, -1), wuv_pair, BF16, "mla_v_absorb")
            (hs,) = _proj(o, [w_o], [F32], res=hs, name="mla_out_s")
            outs["lat_s"].append(lat.reshape(db, 1, MLA_KV_LORA))
            outs["kpe_s"].append(kpe.reshape(db, 1, MLA_ROPE))
        else:
            w_qkv, w_o, scale, nmap = {
                1: (diff_w_qkv, diff_w_o, DIFF_SCALE, 2),
                2: (moba_w_qkv, moba_w_o, ATTN_SCALE, 1),
                3: (swa_w_qkv, swa_w_o, ATTN_SCALE, 1)}[kind]
            w_slots, w_kv, wq = _gqa_weights(w_qkv[l], scale)
            w_qdec = _decode_q_weights(wq, nmap)
            w_ob = w_o[l].astype(BF16)
            w_odec = _decode_o_weights(w_o[l])
            slots, kv = _proj(hp, [w_slots, w_kv], [BF16, F32], g=g_mix, name="qkv_p%d" % kind)
            qdec, kv_s = _proj(hs, [w_qdec, w_kv], [BF16, F32], g=g_mix, name="qkv_s%d" % kind)
            k_p = kv[:, :kw].reshape(b, s, N_KV, HEAD_DIM)
            v_p = kv[:, kw:].reshape(b, s, N_KV, HEAD_DIM)
            k_s, v_s = kv_s[:, :kw], kv_s[:, kw:]
            if kind == 1:
                lam_init = 0.8 - 0.6 * math.exp(-0.3 * i)
                lamv = _pad_lanes(diff_lambda[l].astype(F32))
                gh = _pad_lanes(diff_g_head[l].astype(F32).reshape(1, -1))
                o = _gqa_prompt_attn(slots, bias_causal, b, s, "diff", lamv, gh, lam_init)
                gh4 = jnp.tile(diff_g_head[l].astype(F32).reshape(1, -1), (1, N_KV))
                o_s = _gqa_decode_attn(page_table, qdec, k_s, v_s, jnp.concatenate([bdec, bdec], axis=0),
                                       jnp.concatenate([bself, bself], axis=0), cache_diff_k, cache_diff_v, l,
                                       "diff", lamv, gh4, lam_init)
                pk, pv, sk, sv = "dk_p", "dv_p", "dk_s", "dv_s"
            elif kind == 2:
                o = _gqa_prompt_attn(slots, bias_causal, b, s, "moba")
                o_s = _gqa_decode_attn(page_table, qdec, k_s, v_s, bdec, bself, cache_moba_k, cache_moba_v, l, "moba")
                pk, pv, sk, sv = "mk_p", "mv_p", "mk_s", "mv_s"
            else:
                o = _swa_prompt_attn(slots, bias_swa, swa_sinks[l], b, s)
                sink_rows = jnp.broadcast_to(swa_sinks[l].astype(F32)[:, None], (N_HEADS, LANE))
                o_s = _swa_decode_attn(qdec, state_swa_k[l], state_swa_v[l], k_s, v_s, bdec_swa, bself_swa, sink_rows)
            (hp,) = _proj(o.reshape(b * s, -1), [w_ob], [F32], res=hp, name="attn_out_p%d" % kind)
            (hs,) = _proj(o_s.reshape(db, -1), [w_odec], [F32], res=hs, name="attn_out_s%d" % kind)
            if kind == 3:
                w_buf = state_swa_k.shape[2]
                outs["sk_p"].append(k_p[:, s - w_buf:])
                outs["sv_p"].append(v_p[:, s - w_buf:])
                outs["sk_s"].append(jnp.concatenate([state_swa_k[l], k_s.reshape(db, 1, N_KV, HEAD_DIM)], axis=1)[:, 1:])
                outs["sv_s"].append(jnp.concatenate([state_swa_v[l], v_s.reshape(db, 1, N_KV, HEAD_DIM)], axis=1)[:, 1:])
            else:
                outs[pk].append(k_p.reshape(b, np_pages, PAGE, N_KV, HEAD_DIM))
                outs[pv].append(v_p.reshape(b, np_pages, PAGE, N_KV, HEAD_DIM))
                outs[sk].append(k_s.reshape(db, 1, N_KV, HEAD_DIM))
                outs[sv].append(v_s.reshape(db, 1, N_KV, HEAD_DIM))

        w_up = ffn_w_up[i].astype(BF16)
        w_down = ffn_w_down[i].astype(BF16)
        hp, conv_p = _ffn_prompt(hp, norm_ffn_g[i], w_up, ffn_conv_w[i], ffn_conv_b[i], w_down, b, s)
        hs, conv_s = _ffn_sample(hs, norm_ffn_g[i], w_up, ffn_conv_w[i], ffn_conv_b[i], w_down, state_ffn_conv[i])
        outs["conv_p"].append(conv_p)
        outs["conv_s"].append(conv_s)

    y_prompt = _final_norm(hp, norm_final_g).reshape(b, s, d)
    y_sample = _final_norm(hs, norm_final_g).reshape(db, 1, d)
    st = lambda k: jnp.stack(outs[k])
    return (y_prompt, y_sample,
            st("lat_p"), st("lat_s"), st("kpe_p"), st("kpe_s"),
            st("dk_p"), st("dk_s"), st("dv_p"), st("dv_s"),
            st("mk_p"), st("mk_s"), st("mv_p"), st("mv_s"),
            st("sk_p"), st("sk_s"), st("sv_p"), st("sv_s"),
            st("conv_p"), st("conv_s"))
```

```python
import functools
import math

import jax
import jax.numpy as jnp
import numpy as np
from jax import lax
from jax.experimental import pallas as pl
from jax.experimental.pallas import tpu as pltpu

F32 = jnp.float32
BF16 = jnp.bfloat16

LANE = 128
HEAD_DIM = 64
N_HEADS = 16
N_KV = 4
GROUP = 4
PAGE = 128
MLA_Q_LORA = 384
MLA_KV_LORA = 256
MLA_NOPE = 64
MLA_ROPE = 32
LOG2E = 1.4426950408889634
MLA_SCALE = (MLA_NOPE + MLA_ROPE) ** -0.5
ATTN_SCALE = HEAD_DIM ** -0.5
DIFF_QK = 32
DIFF_SCALE = DIFF_QK ** -0.5
ROPE_THETA = 10000.0
MOBA_BLOCK = 256
MOBA_TOPK = 3
N_BUCKETS = 32
EPS = 1e-6
NEG = -1e30
ONES_LANE = HEAD_DIM
CHUNK = 512
PAGE_UNROLL = 16
FFN_ROWS = 1024
VMEM_LIMIT = 48 * 1024 * 1024

T_DIAG, T_PREV, T_SWA_DIAG, T_SWA_PREV, T_DEC, T_SWA_DEC = range(6)
_SHIFTED_TILES = (T_DIAG, T_PREV, T_DEC)


def _cparams(sem, vmem=None):
    return pltpu.CompilerParams(dimension_semantics=sem, vmem_limit_bytes=vmem)


def _nt_dot(a, b):
    return lax.dot_general(a, b, (((1,), (1,)), ((), ())), preferred_element_type=F32)


def _rms(x, g):
    return x * lax.rsqrt(jnp.mean(x * x, axis=-1, keepdims=True) + EPS) * g


def _bucket_np(dist):
    n = np.maximum(dist, 0)
    nf = np.maximum(n, 1).astype(np.float32)
    large = 16 + (np.log(nf / np.float32(16)) / np.float32(math.log(8.0)) * np.float32(16)).astype(np.int32)
    large = np.minimum(large, N_BUCKETS - 1)
    return np.where(n < 16, n, large).astype(np.int32)


def _bucket_tiles():
    i = np.arange(LANE)[:, None]
    c = np.arange(LANE)[None, :]
    diag = np.where(i >= c, _bucket_np(i - c), -1)
    prev = _bucket_np(LANE + i - c)
    swa_prev = np.where(c >= i, _bucket_np(LANE + i - c), -1)
    dec = _bucket_np(LANE - c + 0 * i)
    return np.stack([diag, prev, diag, swa_prev, dec, dec]).astype(np.int32)


def _bias_tiles_body(rb_ref, d_ref, o_ref):
    h = pl.program_id(0)
    far = rb_ref[N_BUCKETS - 1, h]
    for t in range(6):
        d = d_ref[t]
        acc = jnp.zeros((LANE, LANE), F32)
        for b in range(N_BUCKETS):
            acc = jnp.where(d == b, rb_ref[b, h], acc)
        if t in _SHIFTED_TILES:
            acc = acc - far
        o_ref[t, 0] = jnp.where(d < 0, NEG, acc * LOG2E)


def _bias_tiles(rel_bias):
    d = jnp.asarray(_bucket_tiles())
    return pl.pallas_call(
        _bias_tiles_body,
        grid=(N_HEADS,),
        in_specs=[pl.BlockSpec(memory_space=pltpu.SMEM),
                  pl.BlockSpec((6, LANE, LANE), lambda h: (0, 0, 0))],
        out_specs=pl.BlockSpec((6, 1, LANE, LANE), lambda h: (0, h, 0, 0)),
        out_shape=jax.ShapeDtypeStruct((6, N_HEADS, LANE, LANE), F32),
        compiler_params=_cparams(("arbitrary",)),
        name="bias_tiles",
    )(rel_bias.astype(F32), d)


def _proj_body(*refs, n_w, norm, residual):
    it = iter(refs)
    x_ref = next(it)
    g_ref = next(it) if norm else None
    r_ref = next(it) if residual else None
    w_refs = [next(it) for _ in range(n_w)]
    o_refs = [next(it) for _ in range(n_w)]
    x = x_ref[...]
    if norm:
        x = _rms(x.astype(F32), g_ref[...])
    xb = x.astype(BF16)
    for k in range(n_w):
        acc = jnp.dot(xb, w_refs[k][...], preferred_element_type=F32)
        if residual and k == 0:
            acc = acc + r_ref[...]
        o_refs[k][...] = acc.astype(o_refs[k].dtype)


def _proj(x, ws, out_dtypes, g=None, res=None, tm=512, name="proj"):
    t, kdim = x.shape
    tm = min(tm, t)
    args = [x]
    specs = [pl.BlockSpec((tm, kdim), lambda i: (i, 0))]
    if g is not None:
        args.append(g.reshape(1, kdim).astype(F32))
        specs.append(pl.BlockSpec((1, kdim), lambda i: (0, 0)))
    if res is not None:
        args.append(res)
        specs.append(pl.BlockSpec((tm, res.shape[1]), lambda i: (i, 0)))
    for w in ws:
        args.append(w)
        specs.append(pl.BlockSpec(w.shape, lambda i: (0, 0)))
    outs = pl.pallas_call(
        functools.partial(_proj_body, n_w=len(ws), norm=g is not None, residual=res is not None),
        grid=(t // tm,),
        in_specs=specs,
        out_specs=[pl.BlockSpec((tm, w.shape[1]), lambda i: (i, 0)) for w in ws],
        out_shape=[jax.ShapeDtypeStruct((t, w.shape[1]), dt) for w, dt in zip(ws, out_dtypes)],
        compiler_params=_cparams(("arbitrary",), VMEM_LIMIT),
        name=name,
    )(*args)
    return outs


def _norm_body(x_ref, g_ref, o_ref):
    o_ref[...] = _rms(x_ref[...], g_ref[...])


def _final_norm(x, g, tm=512):
    t, d = x.shape
    tm = min(tm, t)
    return pl.pallas_call(
        _norm_body,
        grid=(t // tm,),
        in_specs=[pl.BlockSpec((tm, d), lambda i: (i, 0)), pl.BlockSpec((1, d), lambda i: (0, 0))],
        out_specs=pl.BlockSpec((tm, d), lambda i: (i, 0)),
        out_shape=jax.ShapeDtypeStruct((t, d), F32),
        compiler_params=_cparams(("arbitrary",)),
        name="final_norm",
    )(x, g.reshape(1, d))


def _headwise_body(x_ref, w_ref, o_ref):
    o_ref[...] = jnp.dot(x_ref[...].astype(BF16), w_ref[0], preferred_element_type=F32).astype(o_ref.dtype)


def _headwise(x, w, out_dtype, name):
    m = x.shape[0]
    n, kb, nb = w.shape
    return pl.pallas_call(
        _headwise_body,
        grid=(n,),
        in_specs=[pl.BlockSpec((m, kb), lambda i: (0, i)), pl.BlockSpec((1, kb, nb), lambda i: (i, 0, 0))],
        out_specs=pl.BlockSpec((m, nb), lambda i: (0, i)),
        out_shape=jax.ShapeDtypeStruct((m, n * nb), out_dtype),
        compiler_params=_cparams(("arbitrary",)),
        name=name,
    )(x, w)


def _mla_stage2_body(s1_ref, gq_ref, gkv_ref, cq_ref, sq_ref, ck_ref, wuq_ref, wuk_ref, wuv_ref,
                     q_ref, k_ref, v_ref, lat_ref, kpe_ref):
    s1 = s1_ref[...]
    cosq, sinq, cosk = cq_ref[...], sq_ref[...], ck_ref[...]
    cqn = _rms(s1[:, :MLA_Q_LORA], gq_ref[...]).astype(BF16)
    qa = jnp.dot(cqn, wuq_ref[...], preferred_element_type=F32)
    for h in range(N_HEADS):
        blk = qa[:, h * LANE:(h + 1) * LANE]
        q_ref[:, h * LANE:(h + 1) * LANE] = (blk * cosq + pltpu.roll(blk, LANE - MLA_ROPE, axis=1) * sinq).astype(BF16)
    lat = _rms(s1[:, MLA_Q_LORA:MLA_Q_LORA + MLA_KV_LORA], gkv_ref[...])
    lat_ref[...] = lat
    latb = lat.astype(BF16)
    pe = s1[:, MLA_Q_LORA + MLA_KV_LORA:]
    kpe = pltpu.roll(pe, 2 * MLA_ROPE, axis=1) * cosk + pltpu.roll(pe, MLA_ROPE, axis=1) * sinq
    kpe_ref[...] = kpe
    kn = jnp.dot(latb, wuk_ref[...], preferred_element_type=F32)
    vn = jnp.dot(latb, wuv_ref[...], preferred_element_type=F32)
    lane = lax.broadcasted_iota(jnp.int32, kpe.shape, 1)
    ones = jnp.where(lane == ONES_LANE, 1.0, 0.0)
    for h in range(N_HEADS):
        k_ref[:, h * LANE:(h + 1) * LANE] = (kn[:, h * LANE:(h + 1) * LANE] + kpe).astype(BF16)
        v_ref[:, h * LANE:(h + 1) * LANE] = (vn[:, h * LANE:(h + 1) * LANE] + ones).astype(BF16)


def _mla_stage2(s1, g_q, g_kv, cosq, sinq, cosk, wuq, wuk, wuv, tm=256):
    t = s1.shape[0]
    tm = min(tm, t)
    ntab = cosq.shape[0] // tm
    row = lambda i: (i, 0)
    tab = lambda i: (i % ntab, 0)
    full = lambda i: (0, 0)
    hw = N_HEADS * LANE
    return pl.pallas_call(
        _mla_stage2_body,
        grid=(t // tm,),
        in_specs=[pl.BlockSpec((tm, s1.shape[1]), row),
                  pl.BlockSpec((1, MLA_Q_LORA), full), pl.BlockSpec((1, MLA_KV_LORA), full),
                  pl.BlockSpec((tm, LANE), tab), pl.BlockSpec((tm, LANE), tab), pl.BlockSpec((tm, LANE), tab),
                  pl.BlockSpec(wuq.shape, full), pl.BlockSpec(wuk.shape, full), pl.BlockSpec(wuv.shape, full)],
        out_specs=[pl.BlockSpec((tm, hw), row), pl.BlockSpec((tm, hw), row), pl.BlockSpec((tm, hw), row),
                   pl.BlockSpec((tm, MLA_KV_LORA), row), pl.BlockSpec((tm, LANE), row)],
        out_shape=[jax.ShapeDtypeStruct((t, hw), BF16), jax.ShapeDtypeStruct((t, hw), BF16),
                   jax.ShapeDtypeStruct((t, hw), BF16),
                   jax.ShapeDtypeStruct((t, MLA_KV_LORA), F32), jax.ShapeDtypeStruct((t, LANE), F32)],
        compiler_params=_cparams(("arbitrary",), VMEM_LIMIT),
        name="mla_stage2",
    )(s1, g_q.reshape(1, -1), g_kv.reshape(1, -1), cosq, sinq, cosk, wuq, wuk, wuv)


QK_LOOKAHEAD = 3


def _flash_scores(qs, ks, s_wr):
    r = qs[0].shape[0]
    for c, q in enumerate(qs):
        s_wr[c * r:(c + 1) * r, :] = _nt_dot(q, ks[c])


def _flash_stage(qs, ks_next, vs, s_sc, m_sc, acc_sc, biases=None):
    n = len(qs)
    r = qs[0].shape[0]
    pending = {}

    def scores(c):
        if ks_next is not None and c < n:
            pending[c] = _nt_dot(qs[c], ks_next[c])

    m_all, acc_all = m_sc[...], acc_sc[...]
    new_m, new_acc = [], []
    for c in range(QK_LOOKAHEAD):
        scores(c)
    for c in range(n):
        s = s_sc[c * r:(c + 1) * r, :]
        if c in pending:
            s_sc[c * r:(c + 1) * r, :] = pending.pop(c)
        if biases is not None:
            s = s + biases[c]
        cols = [s[:, w * LANE:(w + 1) * LANE] for w in range(s.shape[1] // LANE)]
        mx = cols[0]
        for col in cols[1:]:
            mx = jnp.maximum(mx, col)
        m_prev = m_all[c * r:(c + 1) * r]
        m_new = jnp.maximum(m_prev, jnp.max(mx, axis=1, keepdims=True))
        alpha = jnp.exp2(m_prev - m_new)
        p = jnp.concatenate([jnp.exp2(col - m_new) for col in cols], axis=1).astype(BF16)
        new_acc.append(alpha * acc_all[c * r:(c + 1) * r] + jnp.dot(p, vs[c], preferred_element_type=F32))
        new_m.append(m_new)
        scores(c + QK_LOOKAHEAD)
    m_sc[...] = jnp.concatenate(new_m, axis=0)
    acc_sc[...] = jnp.concatenate(new_acc, axis=0)


def _init_flash(m_sc, acc_sc):
    m_sc[...] = jnp.full(m_sc.shape, NEG, F32)
    acc_sc[...] = jnp.zeros(acc_sc.shape, F32)


def _normalised(acc):
    return acc / acc[:, ONES_LANE:ONES_LANE + 1]


def _mla_prompt_body(q_ref, k_ref, v_ref, o_ref, m_sc, acc_sc, s_sc, *, tq, r, nh):
    qi = pl.program_id(2)
    nrb = tq // r
    qs = [q_ref[0, rb * r:(rb + 1) * r, hh * LANE:(hh + 1) * LANE] for hh in range(nh) for rb in range(nrb)]
    _init_flash(m_sc, acc_sc)

    def chunk(ref, start):
        return [ref[0, pl.ds(start, tq), hh * LANE:(hh + 1) * LANE] for hh in range(nh) for _ in range(nrb)]

    _flash_scores(qs, chunk(k_ref, 0), s_sc)

    def far(j, carry):
        _flash_stage(qs, chunk(k_ref, pl.multiple_of((j + 1) * tq, tq)), chunk(v_ref, pl.multiple_of(j * tq, tq)),
                     s_sc, m_sc, acc_sc)
        return carry

    lax.fori_loop(0, qi, far, 0)
    row = lax.broadcasted_iota(jnp.int32, (r, tq), 0)
    col = lax.broadcasted_iota(jnp.int32, (r, tq), 1)
    causal = [jnp.where(col <= row + rb * r, 0.0, NEG) for rb in range(nrb)] * nh
    _flash_stage(qs, None, chunk(v_ref, pl.multiple_of(qi * tq, tq)), s_sc, m_sc, acc_sc, causal)
    lane = lax.broadcasted_iota(jnp.int32, (r, LANE), 1)
    for pr in range(nh // 2):
        for rb in range(nrb):
            left = _normalised(acc_sc[(2 * pr * nrb + rb) * r:(2 * pr * nrb + rb + 1) * r])
            right = _normalised(acc_sc[((2 * pr + 1) * nrb + rb) * r:((2 * pr + 1) * nrb + rb + 1) * r])
            o_ref[0, rb * r:(rb + 1) * r, pr * LANE:(pr + 1) * LANE] = jnp.where(
                lane < HEAD_DIM, left, pltpu.roll(right, HEAD_DIM, axis=1)).astype(o_ref.dtype)


def _mla_prompt_attn(qs, ks, vs, b, s, tq=CHUNK, r=128, nh=4):
    tq = min(tq, s)
    hw = N_HEADS * LANE
    qs, ks, vs = qs.reshape(b, s, hw), ks.reshape(b, s, hw), vs.reshape(b, s, hw)
    nch = nh * (tq // r)
    return pl.pallas_call(
        functools.partial(_mla_prompt_body, tq=tq, r=r, nh=nh),
        grid=(b, N_HEADS // nh, s // tq),
        in_specs=[pl.BlockSpec((1, tq, nh * LANE), lambda bi, hp, qi: (bi, qi, hp)),
                  pl.BlockSpec((1, s, nh * LANE), lambda bi, hp, qi: (bi, 0, hp)),
                  pl.BlockSpec((1, s, nh * LANE), lambda bi, hp, qi: (bi, 0, hp))],
        out_specs=pl.BlockSpec((1, tq, nh * HEAD_DIM), lambda bi, hp, qi: (bi, qi, hp)),
        out_shape=jax.ShapeDtypeStruct((b, s, N_HEADS * HEAD_DIM), BF16),
        scratch_shapes=[pltpu.VMEM((nch * r, LANE), F32)] * 2 + [pltpu.VMEM((nch * r, tq), F32)],
        compiler_params=_cparams(("arbitrary", "arbitrary", "arbitrary"), VMEM_LIMIT),
        name="mla_prompt_attn",
    )(qs, ks, vs)


def _lambda_value(lamv, lam_init):
    a = jnp.sum(lamv[0:1] * lamv[1:2], axis=1, keepdims=True)
    c = jnp.sum(lamv[2:3] * lamv[3:4], axis=1, keepdims=True)
    return jnp.exp(a) - jnp.exp(c) + lam_init


def _with_ones(v):
    lane = lax.broadcasted_iota(jnp.int32, v.shape, 1)
    return jnp.where(lane == ONES_LANE, 1.0, v.astype(F32)).astype(BF16)


def _gqa_prompt_body(*refs, kind, tq, w, lam_init):
    if kind == "diff":
        q_ref, k_ref, v_ref, bias_ref, lamv_ref, gh_ref, o_ref, q_sc, m_sc, acc_sc, s_sc, v_sc = refs
        nmap = 2
    else:
        q_ref, k_ref, v_ref, bias_ref, o_ref, q_sc, m_sc, acc_sc, s_sc, v_sc, k_sc, kmh_sc, kml_sc = refs
        nmap = 1
    qi = pl.program_id(2)
    r = LANE
    nrb = tq // r
    nch = nmap * GROUP * nrb
    s_len = k_ref.shape[1]
    bpc = w // r
    lane_t = lax.broadcasted_iota(jnp.int32, (r, LANE), 1)

    def q_block(hh, rb):
        return q_ref[0, rb * r:(rb + 1) * r, hh * LANE:(hh + 1) * LANE]

    @pl.when(qi == 0)
    def _():
        for n in range(s_len // MOBA_BLOCK):
            sl = slice(n * MOBA_BLOCK, (n + 1) * MOBA_BLOCK)
            v_sc[sl, :] = _with_ones(v_ref[0, sl, :])
            if kind == "moba":
                kb = k_ref[0, sl, :].astype(F32)
                lane_k = lax.broadcasted_iota(jnp.int32, kb.shape, 1)
                k_sc[sl, :] = jnp.where(lane_k == HEAD_DIM + n, 1.0, kb).astype(BF16)
                if n == 0:
                    kmh_sc[...] = jnp.zeros(kmh_sc.shape, BF16)
                    kml_sc[...] = jnp.zeros(kml_sc.shape, BF16)
                km = jnp.sum(kb, axis=0, keepdims=True) * (1.0 / MOBA_BLOCK)
                hi = km.astype(BF16)
                kmh_sc[HEAD_DIM + n:HEAD_DIM + n + 1, :] = hi
                kml_sc[HEAD_DIM + n:HEAD_DIM + n + 1, :] = (km - hi.astype(F32)).astype(BF16)

    if kind == "diff":
        for mp in range(2):
            keep = (lane_t >= mp * DIFF_QK) & (lane_t < (mp + 1) * DIFF_QK)
            for hh in range(GROUP):
                for rb in range(nrb):
                    r0 = ((mp * GROUP + hh) * nrb + rb) * r
                    q_sc[r0:r0 + r, :] = jnp.where(keep, q_block(hh, rb).astype(F32), 0.0).astype(BF16)
    else:
        nblk = s_len // MOBA_BLOCK
        rows = GROUP * tq
        q0 = jnp.concatenate([q_block(hh, rb) for hh in range(GROUP) for rb in range(nrb)], axis=0)
        gate = _nt_dot(q0, kmh_sc[...]) + _nt_dot(q0, kml_sc[...])
        lane_r = lax.broadcasted_iota(jnp.int32, (rows, LANE), 1)
        lane_f = lane_r.astype(F32)
        assert nrb & (nrb - 1) == 0
        row_id = lax.broadcasted_iota(jnp.int32, (rows, LANE), 0)
        row_blk = jnp.bitwise_and(jnp.right_shift(row_id, r.bit_length() - 1), nrb - 1)
        q_blk = jnp.right_shift(qi * tq + row_blk * r, MOBA_BLOCK.bit_length() - 1)
        elig = (lane_r >= HEAD_DIM) & (lane_r < HEAD_DIM + q_blk)
        gsel = jnp.where(elig, gate, NEG)
        pen = jnp.where((lane_r >= HEAD_DIM) & (lane_r < HEAD_DIM + nblk), NEG, 0.0)
        for _ in range(MOBA_TOPK):
            mx = jnp.max(gsel, axis=1, keepdims=True)
            idx = jnp.min(jnp.where(gsel == mx, lane_f, 1e9), axis=1, keepdims=True)
            pick = (lane_f == idx) & (mx > 0.5 * NEG)
            pen = jnp.where(pick, 0.0, pen)
            gsel = jnp.where(pick, NEG, gsel)
        pen = jnp.where(lane_r == HEAD_DIM + q_blk, 0.0, pen)
        q_sc[...] = (q0.astype(F32) + pen).astype(BF16)

    qs = [q_sc[c * r:(c + 1) * r, :] for c in range(nch)]
    _init_flash(m_sc, acc_sc)
    kk_ref = k_sc if kind == "moba" else k_ref.at[0]

    def chunk_bias(j):
        per_block = {}
        for hh in range(GROUP):
            diag = bias_ref[0, 0, hh * r:(hh + 1) * r, :]
            prev = bias_ref[1, 0, hh * r:(hh + 1) * r, :]
            for rb in range(nrb):
                rel0 = j * bpc - (qi * nrb + rb)
                per_block[hh, rb] = jnp.concatenate(
                    [jnp.where(rel0 + t == 0, diag, jnp.where(rel0 + t == -1, prev, jnp.where(rel0 + t > 0, NEG, 0.0)))
                     for t in range(bpc)], axis=1)
        return [per_block[hh, rb] for _ in range(nmap) for hh in range(GROUP) for rb in range(nrb)]

    def stage(j, has_next, biased):
        ks = [kk_ref[pl.ds(pl.multiple_of((j + 1) * w, w), w), :]] * nch if has_next else None
        v = v_sc[pl.ds(pl.multiple_of(j * w, w), w), :]
        _flash_stage(qs, ks, [v] * nch, s_sc, m_sc, acc_sc, chunk_bias(j) if biased else None)

    qb0 = qi * nrb
    jl = (qb0 + nrb - 1) // bpc
    prev_in_earlier = jnp.logical_and(jl >= 1, jl * bpc >= qb0)
    n_plain = jnp.where(prev_in_earlier, jl - 1, jl)

    _flash_scores(qs, [kk_ref[0:w, :]] * nch, s_sc)

    def plain(j, carry):
        stage(j, True, False)
        return carry

    lax.fori_loop(0, n_plain, plain, 0)

    @pl.when(prev_in_earlier)
    def _():
        stage(jl - 1, True, True)

    stage(jl, False, True)

    def chain_rows(ref, mp, hh, rb):
        c = (mp * GROUP + hh) * nrb + rb
        return ref[c * r:(c + 1) * r]

    for rb in range(nrb):
        heads = []
        for hh in range(GROUP):
            if kind == "diff":
                lam = _lambda_value(lamv_ref[...], lam_init)
                a = _normalised(chain_rows(acc_sc, 0, hh, rb)) - lam * _normalised(chain_rows(acc_sc, 1, hh, rb))
                a = jnp.where(lane_t < HEAD_DIM, a, 0.0)
                ms = jnp.sum(a * a, axis=1, keepdims=True) * (1.0 / HEAD_DIM)
                heads.append(a * lax.rsqrt(ms + EPS) * gh_ref[...] * (1.0 - lam_init))
            else:
                heads.append(_normalised(chain_rows(acc_sc, 0, hh, rb)))
        for pr in range(GROUP // 2):
            lane = lax.broadcasted_iota(jnp.int32, (r, LANE), 1)
            o_ref[0, rb * r:(rb + 1) * r, pr * LANE:(pr + 1) * LANE] = jnp.where(
                lane < HEAD_DIM, heads[2 * pr], pltpu.roll(heads[2 * pr + 1], HEAD_DIM, axis=1)).astype(o_ref.dtype)


def _gqa_prompt_attn(slots, bias, b, s, kind, lamv=None, gh=None, lam_init=0.0, w=CHUNK):
    w = min(w, s)
    tq = min(w, 512 if kind == "moba" else 256)
    assert s % w == 0 and w % tq == 0 and tq % LANE == 0 and s % MOBA_BLOCK == 0 and s // MOBA_BLOCK <= 32
    nslot = slots.shape[1] // LANE
    slots = slots.reshape(b, s, nslot * LANE)
    nmap = 2 if kind == "diff" else 1
    rows = nmap * GROUP * tq
    args = [slots, slots, slots, bias]
    specs = [pl.BlockSpec((1, tq, GROUP * LANE), lambda bi, g, qi: (bi, qi, g)),
             pl.BlockSpec((1, s, LANE), lambda bi, g, qi: (bi, 0, N_HEADS + g)),
             pl.BlockSpec((1, s, LANE), lambda bi, g, qi: (bi, 0, N_HEADS + N_KV + g)),
             pl.BlockSpec((2, 1, GROUP * LANE, LANE), lambda bi, g, qi: (0, g, 0, 0))]
    scratch = [pltpu.VMEM((rows, LANE), BF16), pltpu.VMEM((rows, LANE), F32), pltpu.VMEM((rows, LANE), F32),
               pltpu.VMEM((rows, w), F32), pltpu.VMEM((s, LANE), BF16)]
    if kind == "diff":
        args += [lamv, gh]
        specs += [pl.BlockSpec((4, LANE), lambda bi, g, qi: (0, 0)), pl.BlockSpec((1, LANE), lambda bi, g, qi: (0, 0))]
    else:
        scratch += [pltpu.VMEM((s, LANE), BF16), pltpu.VMEM((LANE, LANE), BF16), pltpu.VMEM((LANE, LANE), BF16)]
    return pl.pallas_call(
        functools.partial(_gqa_prompt_body, kind=kind, tq=tq, w=w, lam_init=lam_init),
        grid=(b, N_KV, s // tq),
        in_specs=specs,
        out_specs=pl.BlockSpec((1, tq, GROUP * HEAD_DIM), lambda bi, g, qi: (bi, qi, g)),
        out_shape=jax.ShapeDtypeStruct((b, s, N_HEADS * HEAD_DIM), BF16),
        scratch_shapes=scratch,
        compiler_params=_cparams(("arbitrary", "arbitrary", "arbitrary"), VMEM_LIMIT),
        name=kind + "_prompt_attn",
    )(*args)


def _swa_prompt_body(sink_ref, q_ref, kp_ref, kc_ref, vp_ref, vc_ref, bias_ref, o_ref, *, nq):
    g = pl.program_id(1)
    qi = pl.program_id(2)
    r = LANE
    sink = jnp.concatenate([jnp.full((r, LANE), sink_ref[g * GROUP + hh] * LOG2E, F32) for hh in range(GROUP)], axis=0)

    def tile(ref_prev, ref_cur, t):
        return ref_prev[0] if t == 0 else ref_cur[0, (t - 1) * r:t * r, :]

    scores = []
    for t in range(nq):
        q = jnp.concatenate([q_ref[0, t * r:(t + 1) * r, hh * LANE:(hh + 1) * LANE] for hh in range(GROUP)], axis=0)
        s_prev = _nt_dot(q, tile(kp_ref, kc_ref, t)) + bias_ref[1, 0]
        if t == 0:
            s_prev = jnp.where(qi == 0, NEG, s_prev)
        scores.append((s_prev, _nt_dot(q, kc_ref[0, t * r:(t + 1) * r, :]) + bias_ref[0, 0]))
    for t, (s_prev, s_cur) in enumerate(scores):
        m = jnp.maximum(jnp.max(jnp.maximum(s_prev, s_cur), axis=1, keepdims=True), sink)
        p_prev = jnp.exp2(s_prev - m)
        p_cur = jnp.exp2(s_cur - m)
        den = jnp.sum(p_prev + p_cur, axis=1, keepdims=True) + jnp.exp2(sink - m)
        o = (jnp.dot(p_prev.astype(BF16), tile(vp_ref, vc_ref, t), preferred_element_type=F32)
             + jnp.dot(p_cur.astype(BF16), vc_ref[0, t * r:(t + 1) * r, :], preferred_element_type=F32)) / den
        lane = lax.broadcasted_iota(jnp.int32, (r, LANE), 1)
        for pr in range(GROUP // 2):
            left, right = o[2 * pr * r:(2 * pr + 1) * r], o[(2 * pr + 1) * r:(2 * pr + 2) * r]
            o_ref[0, t * r:(t + 1) * r, pr * LANE:(pr + 1) * LANE] = jnp.where(
                lane < HEAD_DIM, left, pltpu.roll(right, HEAD_DIM, axis=1)).astype(o_ref.dtype)


def _swa_prompt_attn(slots, bias, sinks, b, s, nq=4):
    nslot = slots.shape[1] // LANE
    slots = slots.reshape(b, s, nslot * LANE)
    tq = nq * LANE
    assert s % tq == 0
    prev = lambda off: (lambda bi, g, qi: (bi, jnp.maximum(nq * qi - 1, 0), off + g))
    cur = lambda off: (lambda bi, g, qi: (bi, qi, off + g))
    return pl.pallas_call(
        functools.partial(_swa_prompt_body, nq=nq),
        grid=(b, N_KV, s // tq),
        in_specs=[pl.BlockSpec(memory_space=pltpu.SMEM),
                  pl.BlockSpec((1, tq, GROUP * LANE), lambda bi, g, qi: (bi, qi, g)),
                  pl.BlockSpec((1, LANE, LANE), prev(N_HEADS)), pl.BlockSpec((1, tq, LANE), cur(N_HEADS)),
                  pl.BlockSpec((1, LANE, LANE), prev(N_HEADS + N_KV)), pl.BlockSpec((1, tq, LANE), cur(N_HEADS + N_KV)),
                  pl.BlockSpec((2, 1, GROUP * LANE, LANE), lambda bi, g, qi: (0, g, 0, 0))],
        out_specs=pl.BlockSpec((1, tq, GROUP * HEAD_DIM), lambda bi, g, qi: (bi, qi, g)),
        out_shape=jax.ShapeDtypeStruct((b, s, N_HEADS * HEAD_DIM), BF16),
        compiler_params=_cparams(("arbitrary", "arbitrary", "arbitrary")),
        name="swa_prompt_attn",
    )(sinks.astype(F32), slots, slots, slots, slots, slots, bias)


def _silu(x):
    return x / (1.0 + jnp.exp(-x))


def _ffn_prompt_body(x_ref, g_ref, wg_ref, wu_ref, cwg_ref, cwu_ref, cbg_ref, cbu_ref, wd_ref,
                     o_ref, sg_ref, su_ref, xn_sc, acc_sc, carry_sc, *, tiles_per_seq):
    i = pl.program_id(0)
    j = pl.program_id(1)
    tm = x_ref.shape[0]

    @pl.when(j == 0)
    def _():
        xn_sc[...] = _rms(x_ref[...], g_ref[...]).astype(BF16)
        acc_sc[...] = jnp.zeros(acc_sc.shape, F32)

    first = (i % tiles_per_seq) == 0
    rbs = min(FFN_ROWS, tm)
    row8 = lax.broadcasted_iota(jnp.int32, (8, wg_ref.shape[1]), 0)

    @pl.when(first)
    def _():
        carry_sc[j] = jnp.zeros(carry_sc.shape[1:], F32)

    def up(rb):
        xb = xn_sc[rb * rbs:(rb + 1) * rbs, :]
        return (jnp.dot(xb, wg_ref[...], preferred_element_type=F32),
                jnp.dot(xb, wu_ref[...], preferred_element_type=F32))

    def conv(h, tail, cw_ref, cb_ref):
        p6, p7 = tail[6:7], tail[7:8]
        h1, h2 = pltpu.roll(h, 1, axis=0), pltpu.roll(h, 2, axis=0)
        h1 = jnp.concatenate([jnp.where(row8 == 0, p7, h1[:8]), h1[8:]], axis=0)
        h2 = jnp.concatenate([jnp.where(row8 == 0, p6, jnp.where(row8 == 1, p7, h2[:8])), h2[8:]], axis=0)
        return cb_ref[...] + cw_ref[0:1] * h2 + cw_ref[1:2] * h1 + cw_ref[2:3] * h

    tail_g, tail_u = carry_sc[j, 0], carry_sc[j, 1]
    pending = up(0)
    for rb in range(tm // rbs):
        hg, hu = pending
        if (rb + 1) * rbs < tm:
            pending = up(rb + 1)
        act = _silu(conv(hg, tail_g, cwg_ref, cbg_ref)) * conv(hu, tail_u, cwu_ref, cbu_ref)
        acc_sc[rb * rbs:(rb + 1) * rbs, :] += jnp.dot(act.astype(BF16), wd_ref[...], preferred_element_type=F32)
        tail_g, tail_u = hg[rbs - 8:], hu[rbs - 8:]
    carry_sc[j, 0] = tail_g
    carry_sc[j, 1] = tail_u
    sg_ref[0] = tail_g
    su_ref[0] = tail_u

    @pl.when(j == pl.num_programs(1) - 1)
    def _():
        o_ref[...] = x_ref[...] + acc_sc[...]


def _ffn_prompt(x, g, w_up, conv_w, conv_b, w_down, b, s, tm=1024, fc=256):
    t, d = x.shape
    dff = w_down.shape[0]
    tm = min(tm, s)
    nf = dff // fc
    row = lambda i, j: (i, 0)
    gcol = lambda i, j: (0, j)
    ucol = lambda i, j: (0, nf + j)
    tps = s // tm
    out, sg, su = pl.pallas_call(
        functools.partial(_ffn_prompt_body, tiles_per_seq=tps),
        grid=(t // tm, nf),
        in_specs=[pl.BlockSpec((tm, d), row), pl.BlockSpec((1, d), lambda i, j: (0, 0)),
                  pl.BlockSpec((d, fc), gcol), pl.BlockSpec((d, fc), ucol),
                  pl.BlockSpec((3, fc), gcol), pl.BlockSpec((3, fc), ucol),
                  pl.BlockSpec((1, fc), gcol), pl.BlockSpec((1, fc), ucol),
                  pl.BlockSpec((fc, d), lambda i, j: (j, 0))],
        out_specs=[pl.BlockSpec((tm, d), row),
                   pl.BlockSpec((1, 8, fc), lambda i, j: (i, 0, j)),
                   pl.BlockSpec((1, 8, fc), lambda i, j: (i, 0, j))],
        out_shape=[jax.ShapeDtypeStruct((t, d), F32), jax.ShapeDtypeStruct((t // tm, 8, dff), F32),
                   jax.ShapeDtypeStruct((t // tm, 8, dff), F32)],
        scratch_shapes=[pltpu.VMEM((tm, d), BF16), pltpu.VMEM((tm, d), F32), pltpu.VMEM((nf, 2, 8, fc), F32)],
        compiler_params=_cparams(("arbitrary", "arbitrary"), VMEM_LIMIT),
        name="ffn_prompt",
    )(x, g.reshape(1, d), w_up, w_up, conv_w, conv_w, conv_b.reshape(1, -1), conv_b.reshape(1, -1), w_down)
    last = slice(tps - 1, None, tps)
    state = jnp.concatenate([sg[last, 6:8], su[last, 6:8]], axis=-1)
    return out, state


def _ffn_sample_body(x_ref, g_ref, wg_ref, wu_ref, cwg_ref, cwu_ref, cbg_ref, cbu_ref, wd_ref,
                     s0g_ref, s1g_ref, s0u_ref, s1u_ref, o_ref, hg_ref, hu_ref, xn_sc, acc_sc):
    j = pl.program_id(0)

    @pl.when(j == 0)
    def _():
        xn_sc[...] = _rms(x_ref[...], g_ref[...]).astype(BF16)
        acc_sc[...] = jnp.zeros(acc_sc.shape, F32)

    xb = xn_sc[...]
    hg = jnp.dot(xb, wg_ref[...], preferred_element_type=F32)
    hu = jnp.dot(xb, wu_ref[...], preferred_element_type=F32)
    hg_ref[...] = hg
    hu_ref[...] = hu
    cg = cbg_ref[...] + cwg_ref[0:1] * s0g_ref[...] + cwg_ref[1:2] * s1g_ref[...] + cwg_ref[2:3] * hg
    cu = cbu_ref[...] + cwu_ref[0:1] * s0u_ref[...] + cwu_ref[1:2] * s1u_ref[...] + cwu_ref[2:3] * hu
    acc_sc[...] += jnp.dot((_silu(cg) * cu).astype(BF16), wd_ref[...], preferred_element_type=F32)

    @pl.when(j == pl.num_programs(0) - 1)
    def _():
        o_ref[...] = x_ref[...] + acc_sc[...]


def _ffn_sample(x, g, w_up, conv_w, conv_b, w_down, state, fc=256):
    t, d = x.shape
    dff = w_down.shape[0]
    nf = dff // fc
    s0, s1 = state[:, 0], state[:, 1]
    full = lambda j: (0, 0)
    gcol = lambda j: (0, j)
    ucol = lambda j: (0, nf + j)
    out, hg, hu = pl.pallas_call(
        _ffn_sample_body,
        grid=(nf,),
        in_specs=[pl.BlockSpec((t, d), full), pl.BlockSpec((1, d), full),
                  pl.BlockSpec((d, fc), gcol), pl.BlockSpec((d, fc), ucol),
                  pl.BlockSpec((3, fc), gcol), pl.BlockSpec((3, fc), ucol),
                  pl.BlockSpec((1, fc), gcol), pl.BlockSpec((1, fc), ucol),
                  pl.BlockSpec((fc, d), lambda j: (j, 0)),
                  pl.BlockSpec((t, fc), gcol), pl.BlockSpec((t, fc), gcol),
                  pl.BlockSpec((t, fc), ucol), pl.BlockSpec((t, fc), ucol)],
        out_specs=[pl.BlockSpec((t, d), full), pl.BlockSpec((t, fc), gcol), pl.BlockSpec((t, fc), gcol)],
        out_shape=[jax.ShapeDtypeStruct((t, d), F32), jax.ShapeDtypeStruct((t, dff), F32),
                   jax.ShapeDtypeStruct((t, dff), F32)],
        scratch_shapes=[pltpu.VMEM((t, d), BF16), pltpu.VMEM((t, d), F32)],
        compiler_params=_cparams(("arbitrary",), VMEM_LIMIT),
        name="ffn_sample",
    )(x, g.reshape(1, d), w_up, w_up, conv_w, conv_w, conv_b.reshape(1, -1), conv_b.reshape(1, -1), w_down,
      s0, s1, s0, s1)
    h = jnp.concatenate([hg, hu], axis=-1)
    return out, jnp.stack([s1, h], axis=1)


def _page_copies(pt_ref, seq, slot, n_pages, page_off, pairs):
    def make(p, cache, buf, sem):
        pg = pt_ref[seq, p] + page_off
        return pltpu.make_async_copy(cache.at[pg], buf.at[slot, p], sem.at[slot])

    def start(p, c):
        for cache, buf, sem in pairs:
            make(p, cache, buf, sem).start()
        return c

    def wait(p, c):
        for cache, buf, sem in pairs:
            make(p, cache, buf, sem).wait()
        return c

    return (lambda: lax.fori_loop(0, n_pages, start, 0, unroll=PAGE_UNROLL)), (
        lambda: lax.fori_loop(0, n_pages, wait, 0, unroll=PAGE_UNROLL))


def _prefetch_pages(pt_ref, n_pages, page_off, pairs):
    b = pl.program_id(0)
    nb = pl.num_programs(0)

    @pl.when(b == 0)
    def _():
        _page_copies(pt_ref, 0, 0, n_pages, page_off, pairs)[0]()

    @pl.when(b + 1 < nb)
    def _():
        _page_copies(pt_ref, b + 1, (b + 1) % 2, n_pages, page_off, pairs)[0]()

    slot = b % 2
    _page_copies(pt_ref, b, slot, n_pages, page_off, pairs)[1]()
    return slot


def _softmax_pages(s_sc, p_sc, s_self, n_pages):
    s = s_sc[...]
    m = jnp.maximum(jnp.max(jnp.max(s, axis=0), axis=1, keepdims=True), s_self)
    p = jnp.exp2(s - m[None])
    p_self = jnp.exp2(s_self - m)
    p_sc[...] = p.astype(BF16)
    return p_self, jnp.sum(jnp.sum(p, axis=0), axis=1, keepdims=True) + p_self


def _gqa_decode_body(pt_ref, q_ref, knew_ref, vnew_ref, bdec_ref, bself_ref, *rest,
                     kind, n_pages, page_off, lam_init):
    if kind == "diff":
        lamv_ref, gh_ref, kc_hbm, vc_hbm, o_ref, kbuf, vbuf, s_sc, p_sc, ksem, vsem = rest
    else:
        kc_hbm, vc_hbm, o_ref, kbuf, vbuf, s_sc, p_sc, ksem, vsem = rest
    slot = _prefetch_pages(pt_ref, n_pages, page_off, [(kc_hbm, kbuf, ksem), (vc_hbm, vbuf, vsem)])
    q = q_ref[0]
    rows, kw = q.shape

    if kind == "moba":
        ppb = MOBA_BLOCK // PAGE
        nblk = n_pages // ppb
        lane_k = lax.broadcasted_iota(jnp.int32, (kw, LANE), 1)
        kmt = jnp.zeros((kw, LANE), F32)
        for n in range(nblk):
            blk = kbuf[slot, ppb * n]
            for e in range(1, ppb):
                blk = blk + kbuf[slot, ppb * n + e]
            kmt = jnp.where(lane_k == n, jnp.sum(blk, axis=1, keepdims=True) * (1.0 / MOBA_BLOCK), kmt)
        hi = kmt.astype(BF16)
        lo = (kmt - hi.astype(F32)).astype(BF16)
        gate = jnp.dot(q, hi, preferred_element_type=F32) + jnp.dot(q, lo, preferred_element_type=F32)
        lane_r = lax.broadcasted_iota(jnp.int32, (rows, LANE), 1)
        lane_f = lane_r.astype(F32)
        gsel = jnp.where(lane_r < nblk, gate, NEG)
        pen = jnp.where(lane_r < nblk, NEG, 0.0)
        for _ in range(MOBA_TOPK):
            mx = jnp.max(gsel, axis=1, keepdims=True)
            idx = jnp.min(jnp.where(gsel == mx, lane_f, 1e9), axis=1, keepdims=True)
            pick = (lane_f == idx) & (mx > 0.5 * NEG)
            pen = jnp.where(pick, 0.0, pen)
            gsel = jnp.where(pick, NEG, gsel)

    def scores(p, c):
        s = jnp.dot(q, kbuf[slot, p].astype(BF16), preferred_element_type=F32)
        if kind == "moba":
            s = s + jnp.sum(jnp.where(lane_r == p // ppb, pen, 0.0), axis=1, keepdims=True)
        s_sc[p] = s
        return c

    lax.fori_loop(0, n_pages, scores, 0, unroll=PAGE_UNROLL)
    s_sc[n_pages - 1] = s_sc[n_pages - 1] + bdec_ref[...]
    s_self = jnp.sum(q.astype(F32) * knew_ref[0], axis=1, keepdims=True) + bself_ref[:, 0:1]
    p_self, den = _softmax_pages(s_sc, p_sc, s_self, n_pages)

    def values(p, o):
        return o + _nt_dot(p_sc[p], vbuf[slot, p].astype(BF16))

    on = lax.fori_loop(0, n_pages, values, p_self * vnew_ref[0], unroll=PAGE_UNROLL) / den
    if kind == "diff":
        lam = _lambda_value(lamv_ref[...], lam_init)
        a = on[:N_HEADS] - lam * on[N_HEADS:]
        r16 = lax.broadcasted_iota(jnp.int32, (N_HEADS, kw), 0)
        l16 = lax.broadcasted_iota(jnp.int32, (N_HEADS, kw), 1)
        own = jnp.right_shift(l16, 6) == jnp.right_shift(r16, 2)
        ms = jnp.sum(jnp.where(own, a * a, 0.0), axis=1, keepdims=True) * (1.0 / HEAD_DIM)
        on = a * lax.rsqrt(ms + EPS) * gh_ref[...] * (1.0 - lam_init)
    o_ref[0] = on


def _pos_minor_pages(cache):
    nl, n_pool, page, nkv, hd = cache.shape
    return jnp.transpose(cache, (0, 1, 3, 4, 2)).reshape(nl * n_pool, nkv * hd, page)


def _gqa_decode_attn(page_table, qdec, k_new, v_new, bdec, bself, k_cache, v_cache, layer, kind,
                     lamv=None, gh=None, lam_init=0.0):
    db, n_pages = page_table.shape
    kw = N_KV * HEAD_DIM
    rows = qdec.shape[1] // kw
    n_pool = k_cache.shape[1]
    kc, vc = _pos_minor_pages(k_cache), _pos_minor_pages(v_cache)
    qdec = qdec.reshape(db, rows, kw)
    args = [qdec, k_new.reshape(db, 1, kw), v_new.reshape(db, 1, kw), bdec, bself]
    specs = [pl.BlockSpec((1, rows, kw), lambda b, pt: (b, 0, 0)),
             pl.BlockSpec((1, 1, kw), lambda b, pt: (b, 0, 0)),
             pl.BlockSpec((1, 1, kw), lambda b, pt: (b, 0, 0)),
             pl.BlockSpec((rows, LANE), lambda b, pt: (0, 0)),
             pl.BlockSpec((rows, LANE), lambda b, pt: (0, 0))]
    scratch = [pltpu.VMEM((2, n_pages, kw, PAGE), F32), pltpu.VMEM((2, n_pages, kw, PAGE), F32),
               pltpu.VMEM((n_pages, rows, PAGE), F32), pltpu.VMEM((n_pages, rows, PAGE), BF16)]
    if kind == "diff":
        args += [lamv, gh]
        specs += [pl.BlockSpec((4, LANE), lambda b, pt: (0, 0)), pl.BlockSpec((1, kw), lambda b, pt: (0, 0))]
    else:
        assert n_pages % (MOBA_BLOCK // PAGE) == 0 and n_pages // (MOBA_BLOCK // PAGE) <= LANE
    scratch += [pltpu.SemaphoreType.DMA((2,)), pltpu.SemaphoreType.DMA((2,))]
    args += [kc, vc]
    specs += [pl.BlockSpec(memory_space=pl.ANY), pl.BlockSpec(memory_space=pl.ANY)]
    return pl.pallas_call(
        functools.partial(_gqa_decode_body, kind=kind, n_pages=n_pages, page_off=layer * n_pool, lam_init=lam_init),
        grid_spec=pltpu.PrefetchScalarGridSpec(
            num_scalar_prefetch=1, grid=(db,), in_specs=specs,
            out_specs=pl.BlockSpec((1, N_HEADS, kw), lambda b, pt: (b, 0, 0)),
            scratch_shapes=scratch),
        out_shape=jax.ShapeDtypeStruct((db, N_HEADS, kw), F32),
        compiler_params=_cparams(("arbitrary",), VMEM_LIMIT),
        name=kind + "_decode_attn",
    )(page_table, *args)


def _mla_decode_body(pt_ref, ql_ref, qp_ref, lnew_ref, pnew_ref, lat_hbm, kpe_hbm, o_ref,
                     lbuf, pbuf, s_sc, p_sc, lsem, psem, *, n_pages, page_off):
    slot = _prefetch_pages(pt_ref, n_pages, page_off, [(lat_hbm, lbuf, lsem), (kpe_hbm, pbuf, psem)])
    ql = ql_ref[0]
    qp = qp_ref[0]

    def scores(p, c):
        s_sc[p] = (_nt_dot(ql, lbuf[slot, p].astype(BF16))
                   + jnp.dot(qp, pbuf[slot, p].astype(BF16), preferred_element_type=F32))
        return c

    lax.fori_loop(0, n_pages, scores, 0, unroll=PAGE_UNROLL)
    s_self = (jnp.sum(ql.astype(F32) * lnew_ref[0], axis=1, keepdims=True)
              + jnp.sum(qp.astype(F32) * pnew_ref[0], axis=1, keepdims=True))
    p_self, den = _softmax_pages(s_sc, p_sc, s_self, n_pages)

    def values(p, o):
        return o + jnp.dot(p_sc[p], lbuf[slot, p].astype(BF16), preferred_element_type=F32)

    o_ref[0] = lax.fori_loop(0, n_pages, values, p_self * lnew_ref[0], unroll=PAGE_UNROLL) / den


def _mla_decode_attn(page_table, q_lat, q_pe, lat_new, kpe_new, lat_cache, kpe_cache, layer):
    db, n_pages = page_table.shape
    n_pool = lat_cache.shape[1]
    lc = lat_cache.reshape(lat_cache.shape[0] * n_pool, PAGE, MLA_KV_LORA)
    pc = jnp.transpose(kpe_cache, (0, 1, 3, 2)).reshape(kpe_cache.shape[0] * n_pool, MLA_ROPE, PAGE)
    specs = [pl.BlockSpec((1, N_HEADS, MLA_KV_LORA), lambda b, pt: (b, 0, 0)),
             pl.BlockSpec((1, N_HEADS, MLA_ROPE), lambda b, pt: (b, 0, 0)),
             pl.BlockSpec((1, 1, MLA_KV_LORA), lambda b, pt: (b, 0, 0)),
             pl.BlockSpec((1, 1, MLA_ROPE), lambda b, pt: (b, 0, 0)),
             pl.BlockSpec(memory_space=pl.ANY), pl.BlockSpec(memory_space=pl.ANY)]
    scratch = [pltpu.VMEM((2, n_pages, PAGE, MLA_KV_LORA), F32), pltpu.VMEM((2, n_pages, MLA_ROPE, PAGE), F32),
               pltpu.VMEM((n_pages, N_HEADS, PAGE), F32), pltpu.VMEM((n_pages, N_HEADS, PAGE), BF16),
               pltpu.SemaphoreType.DMA((2,)), pltpu.SemaphoreType.DMA((2,))]
    return pl.pallas_call(
        functools.partial(_mla_decode_body, n_pages=n_pages, page_off=layer * n_pool),
        grid_spec=pltpu.PrefetchScalarGridSpec(
            num_scalar_prefetch=1, grid=(db,), in_specs=specs,
            out_specs=pl.BlockSpec((1, N_HEADS, MLA_KV_LORA), lambda b, pt: (b, 0, 0)),
            scratch_shapes=scratch),
        out_shape=jax.ShapeDtypeStruct((db, N_HEADS, MLA_KV_LORA), F32),
        compiler_params=_cparams(("arbitrary",), VMEM_LIMIT),
        name="mla_decode_attn",
    )(page_table, q_lat, q_pe, lat_new.reshape(db, 1, -1), kpe_new.reshape(db, 1, -1), lc, pc)


def _swa_decode_body(q_ref, k_ref, v_ref, knew_ref, vnew_ref, bdec_ref, bself_ref, sink_ref, o_ref, *, nseq):
    sink = sink_ref[:, 0:1] * LOG2E
    for i in range(nseq):
        q = q_ref[i]
        s = jnp.dot(q, k_ref[i].astype(BF16), preferred_element_type=F32) + bdec_ref[...]
        s_self = jnp.sum(q.astype(F32) * knew_ref[i], axis=1, keepdims=True) + bself_ref[:, 0:1]
        m = jnp.maximum(jnp.maximum(jnp.max(s, axis=1, keepdims=True), s_self), sink)
        p = jnp.exp2(s - m)
        p_self = jnp.exp2(s_self - m)
        den = jnp.sum(p, axis=1, keepdims=True) + p_self + jnp.exp2(sink - m)
        o = _nt_dot(p.astype(BF16), v_ref[i].astype(BF16)) + p_self * vnew_ref[i]
        o_ref[i] = o / den


def _swa_decode_attn(qdec, k_buf, v_buf, k_new, v_new, bdec, bself, sink_rows, nseq=8):
    db, w = k_buf.shape[:2]
    assert w == LANE
    kw = N_KV * HEAD_DIM
    nseq = min(nseq, db)
    qdec = qdec.reshape(db, N_HEADS, kw)
    kt = jnp.transpose(k_buf, (0, 2, 3, 1)).reshape(db, kw, w)
    vt = jnp.transpose(v_buf, (0, 2, 3, 1)).reshape(db, kw, w)
    seq = lambda b: (b, 0, 0)
    full = lambda b: (0, 0)
    return pl.pallas_call(
        functools.partial(_swa_decode_body, nseq=nseq),
        grid=(db // nseq,),
        in_specs=[pl.BlockSpec((nseq, N_HEADS, kw), seq),
                  pl.BlockSpec((nseq, kw, w), seq), pl.BlockSpec((nseq, kw, w), seq),
                  pl.BlockSpec((nseq, 1, kw), seq), pl.BlockSpec((nseq, 1, kw), seq),
                  pl.BlockSpec((N_HEADS, LANE), full), pl.BlockSpec((N_HEADS, LANE), full),
                  pl.BlockSpec((N_HEADS, LANE), full)],
        out_specs=pl.BlockSpec((nseq, N_HEADS, kw), seq),
        out_shape=jax.ShapeDtypeStruct((db, N_HEADS, kw), F32),
        compiler_params=_cparams(("arbitrary",)),
        name="swa_decode_attn",
    )(qdec, kt, vt, k_new.reshape(db, 1, kw), v_new.reshape(db, 1, kw), bdec, bself, sink_rows)


def _rot_cols(w):
    half = w.shape[-1] // 2
    return jnp.concatenate([-w[..., half:], w[..., :half]], axis=-1)


def _pad_lanes(w, width=LANE):
    return jnp.pad(w, [(0, 0)] * (w.ndim - 1) + [(0, width - w.shape[-1])])


def _gqa_weights(w_qkv, scale):
    d = w_qkv.shape[0]
    nq = N_HEADS * HEAD_DIM
    nk = N_KV * HEAD_DIM
    wq = w_qkv[:, :nq].reshape(d, N_HEADS, HEAD_DIM) * (scale * LOG2E)
    wk = w_qkv[:, nq:nq + nk].reshape(d, N_KV, HEAD_DIM)
    wv = w_qkv[:, nq + nk:].reshape(d, N_KV, HEAD_DIM)
    slots = jnp.concatenate([_pad_lanes(wq), _pad_lanes(wk), _pad_lanes(wv)], axis=1)
    return slots.reshape(d, -1).astype(BF16), w_qkv[:, nq:].astype(BF16), wq


def _decode_q_weights(wq, nmap):
    d = wq.shape[0]
    own_group = jnp.eye(N_KV, dtype=wq.dtype).reshape(1, 1, N_KV, 1, N_KV, 1)
    lane_map = (jnp.arange(HEAD_DIM) // (HEAD_DIM // nmap))[None, :] == jnp.arange(nmap)[:, None]
    w = wq.reshape(d, 1, N_KV, GROUP, 1, HEAD_DIM) * lane_map.astype(wq.dtype).reshape(1, nmap, 1, 1, 1, HEAD_DIM)
    return (w * own_group).reshape(d, -1).astype(BF16)


def _decode_o_weights(w_o):
    d = w_o.shape[1]
    own_group = jnp.eye(N_KV, dtype=w_o.dtype).reshape(N_KV, 1, N_KV, 1, 1)
    return (w_o.reshape(N_KV, GROUP, 1, HEAD_DIM, d) * own_group).reshape(-1, d).astype(BF16)


def _rope_tables(pos):
    half = MLA_ROPE // 2
    inv = ROPE_THETA ** (-jnp.arange(half, dtype=F32) / half)
    ang = pos.astype(F32)[:, None] * inv[None, :]
    cos = jnp.concatenate([jnp.cos(ang)] * 2, axis=1)
    sin = jnp.concatenate([jnp.sin(ang)] * 2, axis=1)
    n = pos.shape[0]
    z64, z32, o64 = jnp.zeros((n, 64), F32), jnp.zeros((n, 32), F32), jnp.ones((n, 64), F32)
    cosq = jnp.concatenate([o64, cos, z32], axis=1)
    sinq = jnp.concatenate([z64, sin, z32], axis=1)
    cosk = jnp.concatenate([z64, cos, z32], axis=1)
    return cosq, sinq, cosk


def kernel(x_prompt, x_sample, cache_mla_latent, cache_mla_krope, cache_diff_k, cache_diff_v, cache_moba_k, cache_moba_v, state_swa_k, state_swa_v, state_ffn_conv, page_table, rel_bias, norm_mix_g, norm_ffn_g, norm_final_g, mla_w_dq, mla_g_q, mla_w_uq, mla_w_dkv, mla_g_kv, mla_w_uk, mla_w_uv, mla_w_o, diff_w_qkv, diff_lambda, diff_g_head, diff_w_o, moba_w_qkv, moba_w_o, swa_w_qkv, swa_sinks, swa_w_o, ffn_w_up, ffn_conv_w, ffn_conv_b, ffn_w_down):
    b, s, d = x_prompt.shape
    db = x_sample.shape[0]
    assert x_sample.shape[1] == 1
    depth = norm_mix_g.shape[0]
    past_len = page_table.shape[1] * PAGE
    kw = N_KV * HEAD_DIM
    np_pages = s // PAGE

    tiles = _bias_tiles(rel_bias)
    bias_causal = tiles[T_DIAG:T_PREV + 1].reshape(2, N_KV, GROUP * LANE, LANE)
    bias_swa = tiles[T_SWA_DIAG:T_SWA_PREV + 1].reshape(2, N_KV, GROUP * LANE, LANE)
    bdec = tiles[T_DEC, :, 0, :]
    bself = jnp.broadcast_to(tiles[T_DIAG, :, 0, 0:1], (N_HEADS, LANE))
    bdec_swa = tiles[T_SWA_DEC, :, 0, :]
    bself_swa = jnp.broadcast_to(tiles[T_SWA_DIAG, :, 0, 0:1], (N_HEADS, LANE))

    cos_p = _rope_tables(jnp.arange(s, dtype=jnp.int32))
    cos_s = tuple(jnp.broadcast_to(t, (db, LANE)) for t in _rope_tables(jnp.full((1,), past_len, jnp.int32)))

    hp = x_prompt.reshape(b * s, d)
    hs = x_sample.reshape(db, d)
    outs = {k: [] for k in ("lat_p", "lat_s", "kpe_p", "kpe_s", "dk_p", "dk_s", "dv_p", "dv_s", "mk_p", "mk_s",
                            "mv_p", "mv_s", "sk_p", "sk_s", "sv_p", "sv_s", "conv_p", "conv_s")}

    for i in range(depth):
        kind, l = i % 4, i // 4
        g_mix = norm_mix_g[i]
        if kind == 0:
            w_pe = mla_w_dkv[l][:, MLA_KV_LORA:]
            w1 = jnp.concatenate([mla_w_dq[l], mla_w_dkv[l][:, :MLA_KV_LORA], w_pe, _rot_cols(w_pe),
                                  jnp.zeros((d, LANE - 2 * MLA_ROPE), F32)], axis=1).astype(BF16)
            wuq = mla_w_uq[l] * (MLA_SCALE * LOG2E)
            wuq = jnp.concatenate([wuq, _rot_cols(wuq[..., MLA_NOPE:])], axis=-1).reshape(MLA_Q_LORA, -1).astype(BF16)
            wuk = _pad_lanes(mla_w_uk[l]).reshape(MLA_KV_LORA, -1).astype(BF16)
            wuv = _pad_lanes(mla_w_uv[l]).reshape(MLA_KV_LORA, -1).astype(BF16)
            w_o = mla_w_o[l].astype(BF16)
            (s1,) = _proj(hp, [w1], [F32], g=g_mix, name="mla_down_p")
            qs, ks, vs, lat, kpe = _mla_stage2(s1, mla_g_q[l], mla_g_kv[l], *cos_p, wuq, wuk, wuv)
            o = _mla_prompt_attn(qs, ks, vs, b, s)
            (hp,) = _proj(o.reshape(b * s, -1), [w_o], [F32], res=hp, name="mla_out_p")
            outs["lat_p"].append(lat.reshape(b, np_pages, PAGE, MLA_KV_LORA))
            outs["kpe_p"].append(kpe[:, HEAD_DIM:HEAD_DIM + MLA_ROPE].reshape(b, np_pages, PAGE, MLA_ROPE))
            (s1,) = _proj(hs, [w1], [F32], g=g_mix, name="mla_down_s")
            qs, _, _, lat, kpe = _mla_stage2(s1, mla_g_q[l], mla_g_kv[l], *cos_s, wuq, wuk, wuv)
            kpe = kpe[:, HEAD_DIM:HEAD_DIM + MLA_ROPE]
            wuk_t = jnp.pad(jnp.transpose(mla_w_uk[l], (1, 2, 0)),
                            ((0, 0), (0, LANE - MLA_NOPE), (0, 0))).astype(BF16)
            q_lat = _headwise(qs, wuk_t, BF16, "mla_q_absorb").reshape(db, N_HEADS, MLA_KV_LORA)
            q_pe = qs.reshape(db, N_HEADS, LANE)[:, :, HEAD_DIM:HEAD_DIM + MLA_ROPE]
            o_lat = _mla_decode_attn(page_table, q_lat, q_pe, lat, kpe, cache_mla_latent, cache_mla_krope, l)
            wuv_h = mla_w_uv[l].transpose(1, 0, 2)
            wuv_pair = jnp.concatenate(
                [jnp.concatenate([wuv_h[0::2], jnp.zeros_like(wuv_h[0::2])], axis=-1),
                 jnp.concatenate([jnp.zeros_like(wuv_h[1::2]), wuv_h[1::2]], axis=-1)], axis=1).astype(BF16)
            o = _headwise(o_lat.reshape(db, -1), wuv_pair, BF16, "mla_v_absorb")
            (hs,) = _proj(o, [w_o], [F32], res=hs, name="mla_out_s")
            outs["lat_s"].append(lat.reshape(db, 1, MLA_KV_LORA))
            outs["kpe_s"].append(kpe.reshape(db, 1, MLA_ROPE))
        else:
            w_qkv, w_o, scale, nmap = {
                1: (diff_w_qkv, diff_w_o, DIFF_SCALE, 2),
                2: (moba_w_qkv, moba_w_o, ATTN_SCALE, 1),
                3: (swa_w_qkv, swa_w_o, ATTN_SCALE, 1)}[kind]
            w_slots, w_kv, wq = _gqa_weights(w_qkv[l], scale)
            w_qdec = _decode_q_weights(wq, nmap)
            w_ob = w_o[l].astype(BF16)
            w_odec = _decode_o_weights(w_o[l])
            slots, kv = _proj(hp, [w_slots, w_kv], [BF16, F32], g=g_mix, name="qkv_p%d" % kind)
            qdec, kv_s = _proj(hs, [w_qdec, w_kv], [BF16, F32], g=g_mix, name="qkv_s%d" % kind)
            k_p = kv[:, :kw].reshape(b, s, N_KV, HEAD_DIM)
            v_p = kv[:, kw:].reshape(b, s, N_KV, HEAD_DIM)
            k_s, v_s = kv_s[:, :kw], kv_s[:, kw:]
            if kind == 1:
                lam_init = 0.8 - 0.6 * math.exp(-0.3 * i)
                lamv = _pad_lanes(diff_lambda[l].astype(F32))
                gh = _pad_lanes(diff_g_head[l].astype(F32).reshape(1, -1))
                o = _gqa_prompt_attn(slots, bias_causal, b, s, "diff", lamv, gh, lam_init)
                gh4 = jnp.tile(diff_g_head[l].astype(F32).reshape(1, -1), (1, N_KV))
                o_s = _gqa_decode_attn(page_table, qdec, k_s, v_s, jnp.concatenate([bdec, bdec], axis=0),
                                       jnp.concatenate([bself, bself], axis=0), cache_diff_k, cache_diff_v, l,
                                       "diff", lamv, gh4, lam_init)
                pk, pv, sk, sv = "dk_p", "dv_p", "dk_s", "dv_s"
            elif kind == 2:
                o = _gqa_prompt_attn(slots, bias_causal, b, s, "moba")
                o_s = _gqa_decode_attn(page_table, qdec, k_s, v_s, bdec, bself, cache_moba_k, cache_moba_v, l, "moba")
                pk, pv, sk, sv = "mk_p", "mv_p", "mk_s", "mv_s"
            else:
                o = _swa_prompt_attn(slots, bias_swa, swa_sinks[l], b, s)
                sink_rows = jnp.broadcast_to(swa_sinks[l].astype(F32)[:, None], (N_HEADS, LANE))
                o_s = _swa_decode_attn(qdec, state_swa_k[l], state_swa_v[l], k_s, v_s, bdec_swa, bself_swa, sink_rows)
            (hp,) = _proj(o.reshape(b * s, -1), [w_ob], [F32], res=hp, name="attn_out_p%d" % kind)
            (hs,) = _proj(o_s.reshape(db, -1), [w_odec], [F32], res=hs, name="attn_out_s%d" % kind)
            if kind == 3:
                w_buf = state_swa_k.shape[2]
                outs["sk_p"].append(k_p[:, s - w_buf:])
                outs["sv_p"].append(v_p[:, s - w_buf:])
                outs["sk_s"].append(jnp.concatenate([state_swa_k[l], k_s.reshape(db, 1, N_KV, HEAD_DIM)], axis=1)[:, 1:])
                outs["sv_s"].append(jnp.concatenate([state_swa_v[l], v_s.reshape(db, 1, N_KV, HEAD_DIM)], axis=1)[:, 1:])
            else:
                outs[pk].append(k_p.reshape(b, np_pages, PAGE, N_KV, HEAD_DIM))
                outs[pv].append(v_p.reshape(b, np_pages, PAGE, N_KV, HEAD_DIM))
                outs[sk].append(k_s.reshape(db, 1, N_KV, HEAD_DIM))
                outs[sv].append(v_s.reshape(db, 1, N_KV, HEAD_DIM))

        w_up = ffn_w_up[i].astype(BF16)
        w_down = ffn_w_down[i].astype(BF16)
        hp, conv_p = _ffn_prompt(hp, norm_ffn_g[i], w_up, ffn_conv_w[i], ffn_conv_b[i], w_down, b, s)
        hs, conv_s = _ffn_sample(hs, norm_ffn_g[i], w_up, ffn_conv_w[i], ffn_conv_b[i], w_down, state_ffn_conv[i])
        outs["conv_p"].append(conv_p)
        outs["conv_s"].append(conv_s)

    y_prompt = _final_norm(hp, norm_final_g).reshape(b, s, d)
    y_sample = _final_norm(hs, norm_final_g).reshape(db, 1, d)
    st = lambda k: jnp.stack(outs[k])
    return (y_prompt, y_sample,
            st("lat_p"), st("lat_s"), st("kpe_p"), st("kpe_s"),
            st("dk_p"), st("dk_s"), st("dv_p"), st("dv_s"),
            st("mk_p"), st("mk_s"), st("mv_p"), st("mv_s"),
            st("sk_p"), st("sk_s"), st("sv_p"), st("sv_s"),
            st("conv_p"), st("conv_s"))
```
